```python
import jax, jax.numpy as jnp
from jax import lax
import numpy as np

D_MODEL = 1024
BATCH = 2
SEQ = 8192
DEPTH = 2
DEC_BATCH = 8
DEC_SEQ = 64
PAST_LEN = 2048

CHUNK = 64
N_A_LAYERS = DEPTH // 2
N_B_LAYERS = DEPTH - N_A_LAYERS
POOL_WINDOWS = (2, 4, 8, 16)
N_POOL_GROUPS = len(POOL_WINDOWS)
POOL_GROUP = D_MODEL // N_POOL_GROUPS
POOL_HIST = max(POOL_WINDOWS) - 1
N_HEADS = 16
HEAD_DIM = D_MODEL // N_HEADS
HD = N_HEADS * HEAD_DIM
D_FF = ((8 * D_MODEL // 3 + 127) // 128) * 128
Q_BLOCK = 128
RMS_EPS = 1e-6
FORGET_BIAS_INIT = 2.0

kernel_name = "yoco_pool_fox_streaming_step"


def _rms_norm(x, g):
    xf = x.astype(jnp.float32)
    y = xf * lax.rsqrt(jnp.mean(xf * xf, axis=-1, keepdims=True) + RMS_EPS)
    return (y * g.astype(jnp.float32)).astype(x.dtype)


def _swiglu(h, w_in, w_out):
    g, u = jnp.split(h @ w_in, 2, axis=-1)
    return (jax.nn.silu(g) * u) @ w_out


def _pool_mix(hist, u, pos0, w_pool, scale):
    B, T, D = u.shape
    full = jnp.concatenate([hist.astype(jnp.float32), u.astype(jnp.float32)], axis=1)
    cs = jnp.concatenate([jnp.zeros((B, 1, D), jnp.float32), jnp.cumsum(full, axis=1)], axis=1)
    P = POOL_HIST
    end = cs[:, P + 1:P + 1 + T]
    pos = pos0 + jnp.arange(T)
    outs = []
    for g, w in enumerate(POOL_WINDOWS):
        sl = slice(g * POOL_GROUP, (g + 1) * POOL_GROUP)
        start = cs[:, P + 1 - w:P + 1 - w + T, sl]
        cnt = jnp.minimum(pos + 1, w).astype(jnp.float32)[None, :, None]
        mean = (end[..., sl] - start) / cnt
        outs.append(mean - u[..., sl].astype(jnp.float32))
    d = jnp.stack(outs, axis=2).astype(u.dtype)
    y = jnp.einsum('btgc,gcd->btgd', d, w_pool).reshape(B, T, D)
    return y * scale


def _forget_attention(q, k, v, c_q, c_k):
    B, Tq = q.shape[0], q.shape[1]
    Tk = k.shape[1]
    scale = HEAD_DIM ** -0.5
    k_pos = jnp.arange(Tk)
    q_pos = jnp.arange(Tq) + (Tk - Tq)
    c_kT = jnp.transpose(c_k, (0, 2, 1))

    def block(args):
        qb, cqb, posb = args
        s = jnp.einsum('bqhd,bkhd->bhqk', qb, k, preferred_element_type=jnp.float32) * scale
        decay = jnp.transpose(cqb, (0, 2, 1))[..., None] - c_kT[:, :, None, :]
        mask = k_pos[None, :] <= posb[:, None]
        s = jnp.where(mask, s + decay, -jnp.inf)
        p = jax.nn.softmax(s, axis=-1)
        return jnp.einsum('bhqk,bkhd->bqhd', p.astype(v.dtype), v)

    if Tq <= Q_BLOCK:
        return block((q, c_q, q_pos))
    nb = Tq // Q_BLOCK
    qb = q.reshape(B, nb, Q_BLOCK, N_HEADS, HEAD_DIM).transpose(1, 0, 2, 3, 4)
    cqb = c_q.reshape(B, nb, Q_BLOCK, N_HEADS).transpose(1, 0, 2, 3)
    pb = q_pos.reshape(nb, Q_BLOCK)
    o = lax.map(block, (qb, cqb, pb))
    return o.transpose(1, 0, 2, 3, 4).reshape(B, Tq, N_HEADS, HEAD_DIM)


def _trunk(x, pool_hist, past, weights):
    (ln_ffn1, ln_mix, ln_ffn2, w_ffn_in, w_ffn_out, w_pool, pool_scale,
     ln_kv, w_kv, w_fgate, b_fgate, w_q, w_o, ln_final) = weights
    B, T, _ = x.shape
    pos0 = 0 if past is None else past[0].shape[1]
    new_pool = []
    k_all = v_all = c_k = c_q = None
    k_new = v_new = logf_new = None
    for l in range(DEPTH):
        x = x + 0.5 * _swiglu(_rms_norm(x, ln_ffn1[l]), w_ffn_in[l, 0], w_ffn_out[l, 0])
        u = _rms_norm(x, ln_mix[l])
        if l < N_A_LAYERS:
            hist = pool_hist[l].astype(u.dtype)
            x = x + _pool_mix(hist, u, pos0, w_pool[l], pool_scale[l])
            new_pool.append(jnp.concatenate([hist, u], axis=1)[:, -POOL_HIST:])
        else:
            j = l - N_A_LAYERS
            q = (u @ w_q[j]).reshape(B, T, N_HEADS, HEAD_DIM)
            o = _forget_attention(q, k_all, v_all, c_q, c_k)
            x = x + o.reshape(B, T, HD) @ w_o[j]
        x = x + 0.5 * _swiglu(_rms_norm(x, ln_ffn2[l]), w_ffn_in[l, 1], w_ffn_out[l, 1])
        if l == N_A_LAYERS - 1:
            kv_in = _rms_norm(x, ln_kv)
            kv = kv_in @ w_kv
            k_new = kv[..., :HD].reshape(B, T, N_HEADS, HEAD_DIM)
            v_new = kv[..., HD:].reshape(B, T, N_HEADS, HEAD_DIM)
            logf_new = jax.nn.log_sigmoid((kv_in @ w_fgate).astype(jnp.float32) + b_fgate.astype(jnp.float32))
            if past is None:
                k_all, v_all, logf_all = k_new, v_new, logf_new
            else:
                k_all = jnp.concatenate([past[0].astype(k_new.dtype), k_new], axis=1)
                v_all = jnp.concatenate([past[1].astype(v_new.dtype), v_new], axis=1)
                logf_all = jnp.concatenate([past[2].astype(jnp.float32), logf_new], axis=1)
            c_k = jnp.cumsum(logf_all, axis=1)
            c_q = c_k[:, -T:]
    y = _rms_norm(x, ln_final)
    return y, jnp.stack(new_pool), k_new, v_new, logf_new.astype(x.dtype)


def setup_inputs(seed: int = 0) -> dict:
    key = jax.random.key(seed)
    ks = jax.random.split(key, 24)

    def nrm(k, shape, scale):
        return jax.random.normal(k, shape, jnp.float32) * scale

    def gain(k, shape):
        return 1.0 + nrm(k, shape, 0.05)

    return {
        'x_prompt': nrm(ks[0], (BATCH, SEQ, D_MODEL), 1.0),
        'x_sample': nrm(ks[1], (DEC_BATCH, DEC_SEQ, D_MODEL), 1.0),
        'cache_pool': nrm(ks[2], (N_A_LAYERS, DEC_BATCH, POOL_HIST, D_MODEL), 1.0),
        'cache_k': nrm(ks[3], (DEC_BATCH, PAST_LEN, N_HEADS, HEAD_DIM), 1.0),
        'cache_v': nrm(ks[4], (DEC_BATCH, PAST_LEN, N_HEADS, HEAD_DIM), 1.0),
        'cache_logf': jax.nn.log_sigmoid(FORGET_BIAS_INIT + nrm(ks[5], (DEC_BATCH, PAST_LEN, N_HEADS), 1.0)),
        'ln_ffn1': gain(ks[6], (DEPTH, D_MODEL)),
        'ln_mix': gain(ks[7], (DEPTH, D_MODEL)),
        'ln_ffn2': gain(ks[8], (DEPTH, D_MODEL)),
        'w_ffn_in': nrm(ks[9], (DEPTH, 2, D_MODEL, 2 * D_FF), D_MODEL ** -0.5),
        'w_ffn_out': nrm(ks[10], (DEPTH, 2, D_FF, D_MODEL), D_FF ** -0.5),
        'w_pool': nrm(ks[11], (N_A_LAYERS, N_POOL_GROUPS, POOL_GROUP, POOL_GROUP), POOL_GROUP ** -0.5),
        'pool_scale': gain(ks[12], (N_A_LAYERS, D_MODEL)),
        'ln_kv': gain(ks[13], (D_MODEL,)),
        'w_kv': nrm(ks[14], (D_MODEL, 2 * HD), D_MODEL ** -0.5),
        'w_fgate': nrm(ks[15], (D_MODEL, N_HEADS), D_MODEL ** -0.5),
        'b_fgate': FORGET_BIAS_INIT + nrm(ks[16], (N_HEADS,), 0.1),
        'w_q': nrm(ks[17], (N_B_LAYERS, D_MODEL, HD), D_MODEL ** -0.5),
        'w_o': nrm(ks[18], (N_B_LAYERS, HD, D_MODEL), HD ** -0.5),
        'ln_final': gain(ks[19], (D_MODEL,)),
    }


def reference(x_prompt, x_sample, cache_pool, cache_k, cache_v, cache_logf,
              ln_ffn1, ln_mix, ln_ffn2, w_ffn_in, w_ffn_out, w_pool, pool_scale,
              ln_kv, w_kv, w_fgate, b_fgate, w_q, w_o, ln_final):
    weights = (ln_ffn1, ln_mix, ln_ffn2, w_ffn_in, w_ffn_out, w_pool, pool_scale,
               ln_kv, w_kv, w_fgate, b_fgate, w_q, w_o, ln_final)
    prompt_hist = jnp.zeros((N_A_LAYERS, x_prompt.shape[0], POOL_HIST, D_MODEL), x_prompt.dtype)
    y_prompt, pool_prompt, k_prompt, v_prompt, logf_prompt = _trunk(x_prompt, prompt_hist, None, weights)
    y_sample, pool_sample, k_sample, v_sample, logf_sample = _trunk(
        x_sample, cache_pool, (cache_k, cache_v, cache_logf), weights)
    return (y_prompt, y_sample, pool_prompt, pool_sample, k_prompt, v_prompt, logf_prompt,
            k_sample, v_sample, logf_sample)
```

```python
import functools

import jax
import jax.numpy as jnp
from jax import lax
from jax.experimental import pallas as pl
from jax.experimental.pallas import tpu as pltpu

RMS_EPS = 1e-6
POOL_WINDOWS = (2, 4, 8, 16)
POOL_HIST = max(POOL_WINDOWS) - 1

LANES = 128
MXU_DIM = 256
TOKEN_TILE = 512
FFN_CHUNK = 256
POOL_PAD = 8
POOL_HIST_ROWS = 16
CUMSUM_CHUNK = 256
ATT_TQ = 256
ATT_TK = 512
MASK_VALUE = -1e30
VMEM_LIMIT = 52 * 1024 * 1024

_BF16 = jnp.bfloat16
_F32 = jnp.float32


def _rms(x, g):
    ms = jnp.mean(x * x, axis=-1, keepdims=True)
    return x * lax.rsqrt(ms + RMS_EPS) * g


def _dot(a, b):
    return jnp.dot(a, b, preferred_element_type=_F32)


def _split3(x):
    hi = x.astype(_BF16)
    r = x - hi.astype(_F32)
    mid = r.astype(_BF16)
    lo = (r - mid.astype(_F32)).astype(_BF16)
    return hi, mid, lo


def _params(*sem):
    return pltpu.CompilerParams(dimension_semantics=sem, vmem_limit_bytes=VMEM_LIMIT)


def _const_spec(shape):
    zeros = (0,) * len(shape)
    return pl.BlockSpec(shape, lambda *_: zeros, pipeline_mode=pl.Buffered(1))


def _ffn_body(x_ref, g_ref, win_ref, wout_ref, *rest, d_ff, final_norm):
    if final_norm:
        gf_ref, o_ref, acc_ref = rest
    else:
        o_ref, acc_ref = rest
    x = x_ref[...]
    xn = _rms(x, g_ref[...]).astype(_BF16)
    for c in range(d_ff // FFN_CHUNK):
        lo, hi = c * FFN_CHUNK, (c + 1) * FFN_CHUNK
        g = _dot(xn, win_ref[:, lo:hi])
        u = _dot(xn, win_ref[:, d_ff + lo:d_ff + hi])
        a = (g * (1.0 / (1.0 + jnp.exp(-g))) * u).astype(_BF16)
        part = _dot(a, wout_ref[lo:hi, :])
        if c == 0:
            acc_ref[...] = part
        else:
            acc_ref[...] += part
    y = x + 0.5 * acc_ref[...]
    if final_norm:
        y = _rms(y, gf_ref[...])
    o_ref[...] = y


def _ffn(x, g, w_in, w_out, g_final=None):
    n, d = x.shape
    d_ff = w_out.shape[0]
    assert n % TOKEN_TILE == 0 and d_ff % FFN_CHUNK == 0
    row = pl.BlockSpec((TOKEN_TILE, d), lambda i: (i, 0))
    in_specs = [row, _const_spec((1, d)), _const_spec(w_in.shape), _const_spec(w_out.shape)]
    args = [x, g.reshape(1, d), w_in, w_out]
    if g_final is not None:
        in_specs.append(_const_spec((1, d)))
        args.append(g_final.reshape(1, d))
    return pl.pallas_call(
        functools.partial(_ffn_body, d_ff=d_ff, final_norm=g_final is not None),
        grid=(n // TOKEN_TILE,),
        in_specs=in_specs,
        out_specs=row,
        out_shape=jax.ShapeDtypeStruct((n, d), _F32),
        scratch_shapes=[pltpu.VMEM((TOKEN_TILE, d), _F32)],
        compiler_params=_params("arbitrary"),
        name="ffn",
    )(*args)


def _pool_body(x_ref, hist_ref, g_ref, wp_ref, sc_ref, *rest, tt, pos0, aliased):
    if aliased:
        _, o_ref, pool_ref, buf, s2, s4, s8 = rest
    else:
        o_ref, pool_ref, buf, s2, s4, s8 = rest
    t = pl.program_id(1)
    d = x_ref.shape[-1]
    gw = d // len(POOL_WINDOWS)
    off = POOL_PAD + POOL_HIST_ROWS
    end = off + tt

    @pl.when(t == 0)
    def _():
        zeros = jnp.zeros((POOL_PAD, d), _F32)
        buf[0:POOL_PAD, :] = zeros
        s2[0:POOL_PAD, :] = zeros
        s4[0:POOL_PAD, :] = zeros
        buf[POOL_PAD:off, :] = hist_ref[...]

    @pl.when(t > 0)
    def _():
        buf[POOL_PAD:off, :] = buf[end - POOL_HIST_ROWS:end, :]

    x = x_ref[...]
    u = _rms(x, g_ref[...])
    buf[off:end, :] = u

    s2[POOL_PAD:end, :] = buf[POOL_PAD:end, :] + buf[POOL_PAD - 1:end - 1, :]
    s4[POOL_PAD:end, gw:] = s2[POOL_PAD:end, gw:] + s2[POOL_PAD - 2:end - 2, gw:]
    s8[POOL_PAD:end, 2 * gw:] = s4[POOL_PAD:end, 2 * gw:] + s4[POOL_PAD - 4:end - 4, 2 * gw:]
    sums = (
        s2[off:end, 0:gw],
        s4[off:end, gw:2 * gw],
        s8[off:end, 2 * gw:3 * gw],
        s8[off:end, 3 * gw:] + s8[off - 8:end - 8, 3 * gw:],
    )
    pos = pos0 + t * tt + lax.broadcasted_iota(jnp.int32, (tt, 1), 0)
    for gi, w in enumerate(POOL_WINDOWS):
        sl = slice(gi * gw, (gi + 1) * gw)
        inv_cnt = 1.0 / jnp.minimum(pos + 1, w).astype(_F32)
        diff = (sums[gi] * inv_cnt - u[:, sl]).astype(_BF16)
        y = _dot(diff, wp_ref[gi])
        o_ref[:, sl] = x[:, sl] + y * sc_ref[:, sl]

    @pl.when(t == pl.num_programs(1) - 1)
    def _():
        pool_ref[...] = buf[end - POOL_HIST_ROWS:end, :]


def _pool_mix(x_all, prev_out, hist16, g, w_pool, scale, *, n_streams, t_len, row0, pos0):
    n, d = x_all.shape
    tt = min(TOKEN_TILE, t_len)
    assert t_len % tt == 0 and row0 % tt == 0
    nt = t_len // tt
    blk0 = row0 // tt
    row = pl.BlockSpec((tt, d), lambda b, t: (blk0 + b * nt + t, 0))
    hist = pl.BlockSpec((None, POOL_HIST_ROWS, d), lambda b, t: (b, 0, 0))
    in_specs = [row, hist, _const_spec((1, d)), _const_spec(w_pool.shape), _const_spec((1, d))]
    args = [x_all, hist16, g.reshape(1, d), w_pool, scale.reshape(1, d)]
    aliases = {}
    if prev_out is not None:
        in_specs.append(pl.BlockSpec(memory_space=pl.ANY))
        args.append(prev_out)
        aliases = {5: 0}
    rows_buf = POOL_PAD + POOL_HIST_ROWS + tt
    return pl.pallas_call(
        functools.partial(_pool_body, tt=tt, pos0=pos0, aliased=prev_out is not None),
        grid=(n_streams, nt),
        in_specs=in_specs,
        out_specs=[row, hist],
        out_shape=[jax.ShapeDtypeStruct((n, d), _F32),
                   jax.ShapeDtypeStruct((n_streams, POOL_HIST_ROWS, d), _F32)],
        scratch_shapes=[pltpu.VMEM((rows_buf, d), _F32)] * 4,
        input_output_aliases=aliases,
        compiler_params=_params("arbitrary", "arbitrary"),
        name="pool_mix",
    )(*args)


def _kv_body(x_ref, g_ref, wkv_ref, wf_ref, bf_ref, k_ref, v_ref, lf_ref, kb_ref, vt_ref, *, hd, n_heads):
    xn = _rms(x_ref[...], g_ref[...]).astype(_BF16)
    k = _dot(xn, wkv_ref[:, 0:hd])
    v = _dot(xn, wkv_ref[:, hd:2 * hd])
    k_ref[...] = k
    v_ref[...] = v
    kb_ref[...] = k.astype(_BF16)
    vt_ref[...] = v.T.astype(_BF16)
    z = _dot(xn, wf_ref[...]) + bf_ref[...]
    logf = jnp.minimum(z, 0.0) - jnp.log1p(jnp.exp(-jnp.abs(z)))
    lane = lax.broadcasted_iota(jnp.int32, logf.shape, 1)
    lf_ref[...] = jnp.where(lane < n_heads, logf, 0.0)


def _kv_project(x, g, w_kv, w_fg_pad, b_fg_pad, n_heads):
    n, d = x.shape
    hd = w_kv.shape[1] // 2
    nb = n // TOKEN_TILE
    row = lambda width: pl.BlockSpec((TOKEN_TILE, width), lambda i: (i, 0))
    return pl.pallas_call(
        functools.partial(_kv_body, hd=hd, n_heads=n_heads),
        grid=(nb,),
        in_specs=[row(d), _const_spec((1, d)), _const_spec(w_kv.shape), _const_spec(w_fg_pad.shape),
                  _const_spec((1, LANES))],
        out_specs=[row(hd), row(hd), row(LANES), row(hd),
                   pl.BlockSpec((None, hd, TOKEN_TILE), lambda i: (i, 0, 0))],
        out_shape=[jax.ShapeDtypeStruct((n, hd), _F32), jax.ShapeDtypeStruct((n, hd), _F32),
                   jax.ShapeDtypeStruct((n, LANES), _F32), jax.ShapeDtypeStruct((n, hd), _BF16),
                   jax.ShapeDtypeStruct((nb, hd, TOKEN_TILE), _BF16)],
        compiler_params=_params("arbitrary"),
        name="kv_project",
    )(x, g.reshape(1, d), w_kv, w_fg_pad, b_fg_pad)


def _decay_body(lf_ref, e_ref, *, n_heads):
    n_chunks = lf_ref.shape[0] // CUMSUM_CHUNK
    r = lax.broadcasted_iota(jnp.int32, (CUMSUM_CHUNK, CUMSUM_CHUNK), 0)
    c = lax.broadcasted_iota(jnp.int32, (CUMSUM_CHUNK, CUMSUM_CHUNK), 1)
    tril = jnp.where(r >= c, 1.0, 0.0).astype(_BF16)
    sr = lax.broadcasted_iota(jnp.int32, (LANES, LANES), 0)
    sc = lax.broadcasted_iota(jnp.int32, (LANES, LANES), 1)
    place = [jnp.where((sc == 3 * sr + j) & (sr < n_heads), 1.0, 0.0).astype(_BF16) for j in range(3)]

    def chunk(i, carry):
        rows = pl.ds(pl.multiple_of(i * CUMSUM_CHUNK, CUMSUM_CHUNK), CUMSUM_CHUNK)
        hi, mid, lo = _split3(lf_ref[rows, :])
        csum = _dot(tril, hi) + _dot(tril, mid) + _dot(tril, lo) + carry
        p0, p1, p2 = _split3(-csum)
        feats = _dot(p0, place[0]) + _dot(p1, place[1]) + _dot(p2, place[2])
        e_ref[rows, :] = feats.astype(_BF16)
        return csum[CUMSUM_CHUNK - 1:CUMSUM_CHUNK, :]

    lax.fori_loop(0, n_chunks, chunk, jnp.zeros((1, LANES), _F32))


def _decay_features(logf, n_heads):
    b, t, _ = logf.shape
    assert t % CUMSUM_CHUNK == 0 and 3 * n_heads <= LANES
    blk = pl.BlockSpec((None, t, LANES), lambda i: (i, 0, 0))
    return pl.pallas_call(
        functools.partial(_decay_body, n_heads=n_heads),
        grid=(b,),
        in_specs=[blk],
        out_specs=blk,
        out_shape=jax.ShapeDtypeStruct((b, t, LANES), _BF16),
        compiler_params=_params("arbitrary"),
        name="decay_features",
    )(logf)


def _q_body(x_ref, g_ref, wq_ref, q_ref, *, scale):
    u = _rms(x_ref[...], g_ref[...]).astype(_BF16)
    q_ref[...] = (_dot(u, wq_ref[...]) * scale).astype(_BF16)


def _q_project(x, g, w_q, scale):
    n, d = x.shape
    hd = w_q.shape[1]
    return pl.pallas_call(
        functools.partial(_q_body, scale=scale),
        grid=(n // TOKEN_TILE,),
        in_specs=[pl.BlockSpec((TOKEN_TILE, d), lambda i: (i, 0)), _const_spec((1, d)), _const_spec(w_q.shape)],
        out_specs=pl.BlockSpec((TOKEN_TILE, hd), lambda i: (i, 0)),
        out_shape=jax.ShapeDtypeStruct((n, hd), _BF16),
        compiler_params=_params("arbitrary"),
        name="q_project",
    )(x, g.reshape(1, d), w_q)


def _attn_body(q_ref, k_ref, e_ref, vt_ref, o_ref, *, n_sub, q_off, head_dim):
    pair = pl.program_id(1)
    qblk = pl.program_id(2)
    heads_per_block = LANES // head_dim
    lane = lax.broadcasted_iota(jnp.int32, (ATT_TQ, LANES), 1)
    key_iota = lax.broadcasted_iota(jnp.int32, (ATT_TK, ATT_TQ), 0)
    qry_iota = lax.broadcasted_iota(jnp.int32, (ATT_TK, ATT_TQ), 1)

    for hh in range(heads_per_block):
        head = pair * heads_per_block + hh
        bias_sel = jnp.where((lane >= 3 * head) & (lane < 3 * head + 3), 1.0, 0.0).astype(_BF16)
        head_lanes = (lane >= hh * head_dim) & (lane < (hh + 1) * head_dim)
        for sub in range(n_sub):
            q = q_ref[sub * ATT_TQ:(sub + 1) * ATT_TQ, :]
            q_aug = jnp.concatenate([jnp.where(head_lanes, q, jnp.zeros_like(q)), bias_sel], axis=1)
            q0 = q_off + (qblk * n_sub + sub) * ATT_TQ
            n_full = q0 // ATT_TK

            def kv_step(j, carry, masked):
                m, l, acc = carry
                rows = pl.ds(pl.multiple_of(j * ATT_TK, ATT_TK), ATT_TK)
                k_aug = jnp.concatenate([k_ref[rows, :], e_ref[rows, :]], axis=1)
                s = lax.dot_general(k_aug, q_aug, (((1,), (1,)), ((), ())), preferred_element_type=_F32)
                if masked:
                    s = jnp.where(key_iota + j * ATT_TK <= qry_iota + q0, s, MASK_VALUE)
                m_new = jnp.maximum(m, jnp.max(s, axis=0, keepdims=True))
                alpha = jnp.exp(m - m_new)
                p = jnp.exp(s - m_new)
                l_new = alpha * l + jnp.sum(p, axis=0, keepdims=True)
                vt = vt_ref[j, hh * head_dim:(hh + 1) * head_dim, :]
                acc_new = alpha * acc + _dot(vt, p.astype(_BF16))
                return m_new, l_new, acc_new

            carry = (jnp.full((1, ATT_TQ), MASK_VALUE, _F32), jnp.zeros((1, ATT_TQ), _F32),
                     jnp.zeros((head_dim, ATT_TQ), _F32))
            carry = lax.fori_loop(0, n_full, functools.partial(kv_step, masked=False), carry)
            _, l, acc = kv_step(n_full, carry, True)
            o_ref[hh * head_dim:(hh + 1) * head_dim, sub * ATT_TQ:(sub + 1) * ATT_TQ] = (acc / l).astype(_BF16)


def _attention(q, k, e, vt, *, n_streams, t_q, t_k, q_off, q_row0, k_row0, head_dim):
    hd = q.shape[1]
    n_pairs = hd // LANES
    tq_blk = min(t_q, 4 * ATT_TQ)
    assert t_q % tq_blk == 0 and tq_blk % ATT_TQ == 0 and t_k % ATT_TK == 0 and q_off % ATT_TQ == 0
    assert q_row0 % tq_blk == 0 and k_row0 % t_k == 0 and q_off + t_q <= t_k
    nq = t_q // tq_blk
    qb0, kb0, vb0 = q_row0 // tq_blk, k_row0 // t_k, k_row0 // ATT_TK
    nkb = t_k // ATT_TK
    assert vb0 % nkb == 0
    return pl.pallas_call(
        functools.partial(_attn_body, n_sub=tq_blk // ATT_TQ, q_off=q_off, head_dim=head_dim),
        grid=(n_streams, n_pairs, nq),
        in_specs=[
            pl.BlockSpec((tq_blk, LANES), lambda b, p, i: (qb0 + b * nq + i, p)),
            pl.BlockSpec((t_k, LANES), lambda b, p, i: (kb0 + b, p)),
            pl.BlockSpec((None, t_k, LANES), lambda b, p, i: (b, 0, 0)),
            pl.BlockSpec((nkb, LANES, ATT_TK), lambda b, p, i: (vb0 // nkb + b, p, 0)),
        ],
        out_specs=pl.BlockSpec((LANES, tq_blk), lambda b, p, i: (p, b * nq + i)),
        out_shape=jax.ShapeDtypeStruct((hd, n_streams * t_q), _BF16),
        compiler_params=_params("arbitrary", "arbitrary", "arbitrary"),
        name="forget_attention",
    )(q, k, e, vt)


def _oproj_body(x_ref, ot_ref, wo_ref, y_ref):
    y_ref[...] = x_ref[...] + lax.dot_general(
        ot_ref[...], wo_ref[...], (((0,), (0,)), ((), ())), preferred_element_type=_F32)


def _o_project(x, o_t, w_o):
    n, d = x.shape
    hd = w_o.shape[0]
    row = pl.BlockSpec((TOKEN_TILE, d), lambda i: (i, 0))
    return pl.pallas_call(
        _oproj_body,
        grid=(n // TOKEN_TILE,),
        in_specs=[row, pl.BlockSpec((hd, TOKEN_TILE), lambda i: (0, i)), _const_spec(w_o.shape)],
        out_specs=row,
        out_shape=jax.ShapeDtypeStruct((n, d), _F32),
        compiler_params=_params("arbitrary"),
        name="o_project",
    )(x, o_t, w_o)


def kernel(x_prompt, x_sample, cache_pool, cache_k, cache_v, cache_logf, ln_ffn1, ln_mix, ln_ffn2, w_ffn_in, w_ffn_out, w_pool, pool_scale, ln_kv, w_kv, w_fgate, b_fgate, w_q, w_o, ln_final):
    batch, seq, d = x_prompt.shape
    dec_batch, dec_seq, _ = x_sample.shape
    past_len, n_heads, head_dim = cache_k.shape[1:]
    hd = n_heads * head_dim
    depth = ln_ffn1.shape[0]
    n_a = w_pool.shape[0]
    n_prompt, n_sample = batch * seq, dec_batch * dec_seq
    assert head_dim * 2 == LANES and n_prompt % TOKEN_TILE == 0 and n_sample % TOKEN_TILE == 0
    assert seq % TOKEN_TILE == 0 and TOKEN_TILE == ATT_TK

    win = w_ffn_in.astype(_BF16)
    wout = w_ffn_out.astype(_BF16)
    wpool = w_pool.astype(_BF16)
    wkv = w_kv.astype(_BF16)
    wq = w_q.astype(_BF16)
    wo = w_o.astype(_BF16)
    wfg = jnp.pad(w_fgate, ((0, 0), (0, LANES - n_heads))).astype(_BF16)
    bfg = jnp.pad(b_fgate, (0, LANES - n_heads)).reshape(1, LANES)

    x = jnp.concatenate([x_prompt.reshape(n_prompt, d), x_sample.reshape(n_sample, d)], axis=0)

    sample_q = -(-dec_seq // ATT_TQ) * ATT_TQ
    sample_keys = -(-(past_len + sample_q) // ATT_TK) * ATT_TK
    key_pad = sample_keys - past_len - dec_seq
    assert past_len % ATT_TQ == 0

    new_pool_prompt, new_pool_sample = [], []
    k = v = logf128 = k_bf = vt = e_prompt = e_sample = k_bf_sample = vt_sample = None
    for l in range(depth):
        final = ln_final if l == depth - 1 else None
        x = _ffn(x, ln_ffn1[l], win[l, 0], wout[l, 0])
        if l < n_a:
            hist_p = jnp.zeros((batch, POOL_HIST_ROWS, d), _F32)
            hist_s = jnp.pad(cache_pool[l], ((0, 0), (POOL_HIST_ROWS - POOL_HIST, 0), (0, 0)))
            xp, pool_p = _pool_mix(x, None, hist_p, ln_mix[l], wpool[l], pool_scale[l],
                                   n_streams=batch, t_len=seq, row0=0, pos0=0)
            x, pool_s = _pool_mix(x, xp, hist_s, ln_mix[l], wpool[l], pool_scale[l],
                                  n_streams=dec_batch, t_len=dec_seq, row0=n_prompt, pos0=past_len)
            new_pool_prompt.append(pool_p[:, POOL_HIST_ROWS - POOL_HIST:])
            new_pool_sample.append(pool_s[:, POOL_HIST_ROWS - POOL_HIST:])
        else:
            j = l - n_a
            q_bf = _q_project(x, ln_mix[l], wq[j], head_dim ** -0.5)
            o_prompt = _attention(q_bf, k_bf, e_prompt, vt, n_streams=batch, t_q=seq, t_k=seq, q_off=0,
                                  q_row0=0, k_row0=0, head_dim=head_dim)
            q_s = jnp.pad(q_bf[n_prompt:].reshape(dec_batch, dec_seq, hd),
                          ((0, 0), (0, sample_q - dec_seq), (0, 0))).reshape(dec_batch * sample_q, hd)
            o_sample = _attention(q_s, k_bf_sample, e_sample, vt_sample, n_streams=dec_batch, t_q=sample_q,
                                  t_k=sample_keys, q_off=past_len, q_row0=0, k_row0=0, head_dim=head_dim)
            o_sample = o_sample.reshape(hd, dec_batch, sample_q)[:, :, :dec_seq].reshape(hd, n_sample)
            x = _o_project(x, jnp.concatenate([o_prompt, o_sample], axis=1), wo[j])
        x = _ffn(x, ln_ffn2[l], win[l, 1], wout[l, 1], final)
        if l == n_a - 1:
            k, v, logf128, k_bf, vt = _kv_project(x, ln_kv, wkv, wfg, bfg, n_heads)
            e_prompt = _decay_features(logf128[:n_prompt].reshape(batch, seq, LANES), n_heads)
            logf_s = jnp.concatenate([
                jnp.pad(cache_logf, ((0, 0), (0, 0), (0, LANES - n_heads))),
                logf128[n_prompt:].reshape(dec_batch, dec_seq, LANES),
                jnp.zeros((dec_batch, key_pad, LANES), _F32)], axis=1)
            e_sample = _decay_features(logf_s, n_heads)
            k_s = jnp.concatenate([
                cache_k.reshape(dec_batch, past_len, hd).astype(_BF16),
                k_bf[n_prompt:].reshape(dec_batch, dec_seq, hd),
                jnp.zeros((dec_batch, key_pad, hd), _BF16)], axis=1)
            v_s = jnp.concatenate([
                cache_v.reshape(dec_batch, past_len, hd).astype(_BF16),
                v[n_prompt:].reshape(dec_batch, dec_seq, hd).astype(_BF16),
                jnp.zeros((dec_batch, key_pad, hd), _BF16)], axis=1)
            k_bf_sample = k_s.reshape(dec_batch * sample_keys, hd)
            vt_sample = v_s.reshape(dec_batch * sample_keys // ATT_TK, ATT_TK, hd).transpose(0, 2, 1)

    def split(a, tail):
        return (a[:n_prompt].reshape((batch, seq) + tail), a[n_prompt:].reshape((dec_batch, dec_seq) + tail))

    y_prompt, y_sample = split(x, (d,))
    k_prompt, k_sample = split(k, (n_heads, head_dim))
    v_prompt, v_sample = split(v, (n_heads, head_dim))
    logf_prompt, logf_sample = split(logf128[:, :n_heads], (n_heads,))
    return (y_prompt, y_sample, jnp.stack(new_pool_prompt), jnp.stack(new_pool_sample),
            k_prompt, v_prompt, logf_prompt, k_sample, v_sample, logf_sample)
```

```python
import functools

import jax
import jax.numpy as jnp
from jax import lax
from jax.experimental import pallas as pl
from jax.experimental.pallas import tpu as pltpu

RMS_EPS = 1e-6
POOL_WINDOWS = (2, 4, 8, 16)
POOL_HIST = max(POOL_WINDOWS) - 1

LANES = 128
TOKEN_TILE = 512
FFN_CHUNK = 256
POOL_PAD = 8
POOL_HIST_ROWS = 16
CUMSUM_CHUNK = 256
ATT_TQ = 256
ATT_TK = 512
SUM_ROWS = 16
MASK_VALUE = -1e30
LOG2_E = 1.4426950408889634
VMEM_LIMIT = 52 * 1024 * 1024

_BF16 = jnp.bfloat16
_F32 = jnp.float32


def _rms(x, g):
    ms = jnp.mean(x * x, axis=-1, keepdims=True)
    return x * lax.rsqrt(ms + RMS_EPS) * g


def _dot(a, b):
    return jnp.dot(a, b, preferred_element_type=_F32)


def _split3(x):
    hi = x.astype(_BF16)
    r = x - hi.astype(_F32)
    mid = r.astype(_BF16)
    lo = (r - mid.astype(_F32)).astype(_BF16)
    return hi, mid, lo


def _params(*sem):
    return pltpu.CompilerParams(dimension_semantics=sem, vmem_limit_bytes=VMEM_LIMIT)


def _const_spec(shape):
    zeros = (0,) * len(shape)
    return pl.BlockSpec(shape, lambda *_: zeros, pipeline_mode=pl.Buffered(1))


def _slab_blocks(rows):
    assert all(n % TOKEN_TILE == 0 for n in rows)
    return [n // TOKEN_TILE for n in rows]


def _slab_specs(rows, width, transposed=False):
    specs, start = [], 0
    for nb in _slab_blocks(rows):
        def index(i, start=start, nb=nb):
            blk = jnp.clip(i - start, 0, nb - 1)
            return (0, blk) if transposed else (blk, 0)
        shape = (width, TOKEN_TILE) if transposed else (TOKEN_TILE, width)
        specs.append(pl.BlockSpec(shape, index))
        start += nb
    return specs


def _slab_read(refs, rows):
    i = pl.program_id(0)
    blocks = _slab_blocks(rows)
    value, end = refs[-1][...], sum(blocks)
    for ref, nb in zip(refs[-2::-1], blocks[:0:-1]):
        end -= nb
        value = jnp.where(i < end, ref[...], value)
    return value


def _slab_write(refs, rows, value):
    i = pl.program_id(0)
    start = 0
    for ref, nb in zip(refs, _slab_blocks(rows)):
        @pl.when((i >= start) & (i < start + nb))
        def _(ref=ref):
            ref[...] = value
        start += nb


def _slab_shapes(rows, width, dtype):
    return [jax.ShapeDtypeStruct((n, width), dtype) for n in rows]


def _ffn_body(*refs, rows_in, rows_out, d_ff, final_norm):
    x_refs, refs = refs[:len(rows_in)], refs[len(rows_in):]
    g_ref, win_ref, wout_ref = refs[:3]
    refs = refs[3:]
    if final_norm:
        gf_ref, refs = refs[0], refs[1:]
    o_refs, acc_ref = refs[:len(rows_out)], refs[len(rows_out)]
    x = _slab_read(x_refs, rows_in)
    xn = _rms(x, g_ref[...]).astype(_BF16)
    for c in range(d_ff // FFN_CHUNK):
        lo, hi = c * FFN_CHUNK, (c + 1) * FFN_CHUNK
        g = _dot(xn, win_ref[:, lo:hi])
        u = _dot(xn, win_ref[:, d_ff + lo:d_ff + hi])
        a = (g * (1.0 / (1.0 + jnp.exp(-g))) * u).astype(_BF16)
        part = _dot(a, wout_ref[lo:hi, :])
        if c == 0:
            acc_ref[...] = part
        else:
            acc_ref[...] += part
    y = x + 0.5 * acc_ref[...]
    if final_norm:
        y = _rms(y, gf_ref[...])
    _slab_write(o_refs, rows_out, y)


def _ffn(xs, g, w_in, w_out, g_final=None, out_rows=None):
    d = xs[0].shape[1]
    d_ff = w_out.shape[0]
    assert d_ff % FFN_CHUNK == 0
    rows_in = tuple(x.shape[0] for x in xs)
    rows_out = tuple(out_rows) if out_rows else (sum(rows_in),)
    assert sum(rows_out) == sum(rows_in)
    in_specs = _slab_specs(rows_in, d) + [_const_spec((1, d)), _const_spec(w_in.shape), _const_spec(w_out.shape)]
    args = list(xs) + [g.reshape(1, d), w_in, w_out]
    if g_final is not None:
        in_specs.append(_const_spec((1, d)))
        args.append(g_final.reshape(1, d))
    out = pl.pallas_call(
        functools.partial(_ffn_body, rows_in=rows_in, rows_out=rows_out, d_ff=d_ff,
                          final_norm=g_final is not None),
        grid=(sum(_slab_blocks(rows_in)),),
        in_specs=in_specs,
        out_specs=_slab_specs(rows_out, d),
        out_shape=_slab_shapes(rows_out, d, _F32),
        scratch_shapes=[pltpu.VMEM((TOKEN_TILE, d), _F32)],
        compiler_params=_params("arbitrary"),
        name="ffn",
    )(*args)
    return out if out_rows else out[0]


def _pool_body(x_ref, hist_ref, g_ref, wp_ref, sc_ref, *rest, tt, pos0, aliased):
    if aliased:
        _, o_ref, pool_ref, buf, s2, s4, s8 = rest
    else:
        o_ref, pool_ref, buf, s2, s4, s8 = rest
    t = pl.program_id(1)
    d = x_ref.shape[-1]
    gw = d // len(POOL_WINDOWS)
    off = POOL_PAD + POOL_HIST_ROWS
    end = off + tt

    @pl.when(t == 0)
    def _():
        zeros = jnp.zeros((POOL_PAD, d), _F32)
        buf[0:POOL_PAD, :] = zeros
        s2[0:POOL_PAD, :] = zeros
        s4[0:POOL_PAD, :] = zeros
        buf[POOL_PAD:off, :] = hist_ref[...]

    @pl.when(t > 0)
    def _():
        buf[POOL_PAD:off, :] = buf[end - POOL_HIST_ROWS:end, :]

    x = x_ref[...]
    u = _rms(x, g_ref[...])
    buf[off:end, :] = u

    s2[POOL_PAD:end, :] = buf[POOL_PAD:end, :] + buf[POOL_PAD - 1:end - 1, :]
    s4[POOL_PAD:end, gw:] = s2[POOL_PAD:end, gw:] + s2[POOL_PAD - 2:end - 2, gw:]
    s8[POOL_PAD:end, 2 * gw:] = s4[POOL_PAD:end, 2 * gw:] + s4[POOL_PAD - 4:end - 4, 2 * gw:]
    sums = (
        s2[off:end, 0:gw],
        s4[off:end, gw:2 * gw],
        s8[off:end, 2 * gw:3 * gw],
        s8[off:end, 3 * gw:] + s8[off - 8:end - 8, 3 * gw:],
    )
    pos = pos0 + t * tt + lax.broadcasted_iota(jnp.int32, (tt, 1), 0)
    for gi, w in enumerate(POOL_WINDOWS):
        sl = slice(gi * gw, (gi + 1) * gw)
        inv_cnt = 1.0 / jnp.minimum(pos + 1, w).astype(_F32)
        diff = (sums[gi] * inv_cnt - u[:, sl]).astype(_BF16)
        y = _dot(diff, wp_ref[gi])
        o_ref[:, sl] = x[:, sl] + y * sc_ref[:, sl]

    @pl.when(t == pl.num_programs(1) - 1)
    def _():
        pool_ref[...] = buf[end - POOL_HIST_ROWS:end, :]


def _pool_mix(x_all, prev_out, hist16, g, w_pool, scale, *, n_streams, t_len, row0, pos0):
    n, d = x_all.shape
    tt = min(TOKEN_TILE, t_len)
    assert t_len % tt == 0 and row0 % tt == 0
    nt = t_len // tt
    blk0 = row0 // tt
    row = pl.BlockSpec((tt, d), lambda b, t: (blk0 + b * nt + t, 0))
    hist = pl.BlockSpec((None, POOL_HIST_ROWS, d), lambda b, t: (b, 0, 0))
    in_specs = [row, hist, _const_spec((1, d)), _const_spec(w_pool.shape), _const_spec((1, d))]
    args = [x_all, hist16, g.reshape(1, d), w_pool, scale.reshape(1, d)]
    aliases = {}
    if prev_out is not None:
        in_specs.append(pl.BlockSpec(memory_space=pl.ANY))
        args.append(prev_out)
        aliases = {5: 0}
    rows_buf = POOL_PAD + POOL_HIST_ROWS + tt
    return pl.pallas_call(
        functools.partial(_pool_body, tt=tt, pos0=pos0, aliased=prev_out is not None),
        grid=(n_streams, nt),
        in_specs=in_specs,
        out_specs=[row, hist],
        out_shape=[jax.ShapeDtypeStruct((n, d), _F32),
                   jax.ShapeDtypeStruct((n_streams, POOL_HIST_ROWS, d), _F32)],
        scratch_shapes=[pltpu.VMEM((rows_buf, d), _F32)] * 4,
        input_output_aliases=aliases,
        compiler_params=_params("arbitrary", "arbitrary"),
        name="pool_mix",
    )(*args)


def _kv_body(x_ref, g_ref, wkv_ref, wf_ref, bf_ref, *out_refs, rows, hd, n_heads):
    ns = len(rows)
    k_refs, v_refs = out_refs[:ns], out_refs[ns:2 * ns]
    lf_ref, kb_ref, vt_ref = out_refs[2 * ns:]
    xn = _rms(x_ref[...], g_ref[...]).astype(_BF16)
    k = _dot(xn, wkv_ref[:, 0:hd])
    v = _dot(xn, wkv_ref[:, hd:2 * hd])
    _slab_write(k_refs, rows, k)
    _slab_write(v_refs, rows, v)
    kb_ref[...] = k.astype(_BF16)
    vt_ref[...] = v.T.astype(_BF16)
    z = _dot(xn, wf_ref[...]) + bf_ref[...]
    logf = jnp.minimum(z, 0.0) - jnp.log1p(jnp.exp(-jnp.abs(z)))
    lane = lax.broadcasted_iota(jnp.int32, logf.shape, 1)
    lf_ref[...] = jnp.where(lane < n_heads, logf, 0.0)


def _kv_project(x, rows, g, w_kv, w_fg_pad, b_fg_pad, n_heads):
    n, d = x.shape
    hd = w_kv.shape[1] // 2
    nb = n // TOKEN_TILE
    row = lambda width: pl.BlockSpec((TOKEN_TILE, width), lambda i: (i, 0))
    out = pl.pallas_call(
        functools.partial(_kv_body, rows=rows, hd=hd, n_heads=n_heads),
        grid=(nb,),
        in_specs=[row(d), _const_spec((1, d)), _const_spec(w_kv.shape), _const_spec(w_fg_pad.shape),
                  _const_spec((1, LANES))],
        out_specs=_slab_specs(rows, hd) + _slab_specs(rows, hd) + [
            row(LANES), row(hd), pl.BlockSpec((None, hd, TOKEN_TILE), lambda i: (i, 0, 0))],
        out_shape=_slab_shapes(rows, hd, _F32) + _slab_shapes(rows, hd, _F32) + [
            jax.ShapeDtypeStruct((n, LANES), _F32), jax.ShapeDtypeStruct((n, hd), _BF16),
            jax.ShapeDtypeStruct((nb, hd, TOKEN_TILE), _BF16)],
        compiler_params=_params("arbitrary"),
        name="kv_project",
    )(x, g.reshape(1, d), w_kv, w_fg_pad, b_fg_pad)
    ns = len(rows)
    return out[:ns], out[ns:2 * ns], out[2 * ns], out[2 * ns + 1], out[2 * ns + 2]


def _cache_body(ck_ref, cv_ref, kn_ref, vn_ref, k_ref, vt_ref, *, n_cache_blocks, dec_seq):
    j = pl.program_id(1)

    @pl.when(j < n_cache_blocks)
    def _():
        k_ref[...] = ck_ref[...].astype(_BF16)
        vt_ref[...] = cv_ref[...].T.astype(_BF16)

    @pl.when(j >= n_cache_blocks)
    def _():
        hd = k_ref.shape[1]
        pad = jnp.zeros((ATT_TK - dec_seq, hd), _F32)
        k_ref[...] = jnp.concatenate([kn_ref[...].astype(_F32), pad], axis=0).astype(_BF16)
        vt_ref[...] = jnp.concatenate([vn_ref[...], pad], axis=0).T.astype(_BF16)


def _cache_keys(cache_k, cache_v, k_new_bf, v_new, *, new_row0, dec_seq):
    s, past, hd = cache_k.shape
    assert past % ATT_TK == 0 and dec_seq % 16 == 0 and dec_seq <= ATT_TK and new_row0 % dec_seq == 0
    ncb = past // ATT_TK
    nkb = ncb + 1
    cache = pl.BlockSpec((None, ATT_TK, hd), lambda b, j: (b, jnp.minimum(j, ncb - 1), 0))
    return pl.pallas_call(
        functools.partial(_cache_body, n_cache_blocks=ncb, dec_seq=dec_seq),
        grid=(s, nkb),
        in_specs=[cache, cache,
                  pl.BlockSpec((dec_seq, hd), lambda b, j: (new_row0 // dec_seq + b, 0)),
                  pl.BlockSpec((dec_seq, hd), lambda b, j: (b, 0))],
        out_specs=[pl.BlockSpec((ATT_TK, hd), lambda b, j: (b * nkb + j, 0)),
                   pl.BlockSpec((None, hd, ATT_TK), lambda b, j: (b * nkb + j, 0, 0))],
        out_shape=[jax.ShapeDtypeStruct((s * nkb * ATT_TK, hd), _BF16),
                   jax.ShapeDtypeStruct((s * nkb, hd, ATT_TK), _BF16)],
        compiler_params=_params("arbitrary", "arbitrary"),
        name="cache_keys",
    )(cache_k, cache_v, k_new_bf, v_new)


def _decay_body(lf_ref, e_ref, *, n_heads):
    n_chunks = lf_ref.shape[0] // CUMSUM_CHUNK
    r = lax.broadcasted_iota(jnp.int32, (CUMSUM_CHUNK, CUMSUM_CHUNK), 0)
    c = lax.broadcasted_iota(jnp.int32, (CUMSUM_CHUNK, CUMSUM_CHUNK), 1)
    tril = jnp.where(r >= c, 1.0, 0.0).astype(_BF16)
    sr = lax.broadcasted_iota(jnp.int32, (LANES, LANES), 0)
    sc = lax.broadcasted_iota(jnp.int32, (LANES, LANES), 1)
    place = [jnp.where((sc == 3 * sr + j) & (sr < n_heads), 1.0, 0.0).astype(_BF16) for j in range(3)]

    def chunk(i, carry):
        rows = pl.ds(pl.multiple_of(i * CUMSUM_CHUNK, CUMSUM_CHUNK), CUMSUM_CHUNK)
        hi, mid, lo = _split3(lf_ref[rows, :])
        csum = _dot(tril, hi) + _dot(tril, mid) + _dot(tril, lo) + carry
        p0, p1, p2 = _split3(-LOG2_E * csum)
        feats = _dot(p0, place[0]) + _dot(p1, place[1]) + _dot(p2, place[2])
        e_ref[rows, :] = feats.astype(_BF16)
        return csum[CUMSUM_CHUNK - 1:CUMSUM_CHUNK, :]

    lax.fori_loop(0, n_chunks, chunk, jnp.zeros((1, LANES), _F32))


def _decay_features(logf, n_heads):
    b, t, _ = logf.shape
    assert t % CUMSUM_CHUNK == 0 and 3 * n_heads <= LANES
    blk = pl.BlockSpec((None, t, LANES), lambda i: (i, 0, 0))
    return pl.pallas_call(
        functools.partial(_decay_body, n_heads=n_heads),
        grid=(b,),
        in_specs=[blk],
        out_specs=blk,
        out_shape=jax.ShapeDtypeStruct((b, t, LANES), _BF16),
        compiler_params=_params("arbitrary"),
        name="decay_features",
    )(logf)


def _q_body(x_ref, g_ref, wq_ref, q_ref, *, scale):
    u = _rms(x_ref[...], g_ref[...]).astype(_BF16)
    q_ref[...] = (_dot(u, wq_ref[...]) * scale).astype(_BF16)


def _q_project(x, g, w_q, scale):
    n, d = x.shape
    hd = w_q.shape[1]
    return pl.pallas_call(
        functools.partial(_q_body, scale=scale),
        grid=(n // TOKEN_TILE,),
        in_specs=[pl.BlockSpec((TOKEN_TILE, d), lambda i: (i, 0)), _const_spec((1, d)), _const_spec(w_q.shape)],
        out_specs=pl.BlockSpec((TOKEN_TILE, hd), lambda i: (i, 0)),
        out_shape=jax.ShapeDtypeStruct((n, hd), _BF16),
        compiler_params=_params("arbitrary"),
        name="q_project",
    )(x, g.reshape(1, d), w_q)


def _attn_group(n_sub, q_off):
    return ATT_TK // ATT_TQ if (n_sub * ATT_TQ) % ATT_TK == 0 and q_off % ATT_TK == 0 else 1


def _attn_body(q_ref, k_ref, e_ref, vt_ref, o_ref, s_ref, *, n_sub, q_off, head_dim):
    pair = pl.program_id(1)
    qblk = pl.program_id(2)
    heads_per_block = LANES // head_dim
    group = _attn_group(n_sub, q_off)
    lane = lax.broadcasted_iota(jnp.int32, (ATT_TQ, LANES), 1)
    key_iota = lax.broadcasted_iota(jnp.int32, (ATT_TK, ATT_TQ), 0)
    qry_iota = lax.broadcasted_iota(jnp.int32, (ATT_TK, ATT_TQ), 1)
    chains = [(hh, s) for hh in range(heads_per_block) for s in range(group)]
    ones_rows = jnp.ones((SUM_ROWS, ATT_TK), _BF16)

    for g in range(n_sub // group):
        q0 = q_off + (qblk * n_sub + g * group) * ATT_TQ
        n_full = q0 // ATT_TK
        q_augs = []
        for hh, s in chains:
            head = pair * heads_per_block + hh
            bias_sel = jnp.where((lane >= 3 * head) & (lane < 3 * head + 3), 1.0, 0.0).astype(_BF16)
            head_lanes = (lane >= hh * head_dim) & (lane < (hh + 1) * head_dim)
            q = q_ref[(g * group + s) * ATT_TQ:(g * group + s + 1) * ATT_TQ, :]
            q_augs.append(jnp.concatenate([jnp.where(head_lanes, q, jnp.zeros_like(q)), bias_sel], axis=1))

        def scores(j, slot, masked):
            rows = pl.ds(pl.multiple_of(j * ATT_TK, ATT_TK), ATT_TK)
            k_aug = jnp.concatenate([k_ref[rows, :], e_ref[rows, :]], axis=1)
            tiles = [lax.dot_general(k_aug, q_aug, (((1,), (1,)), ((), ())), preferred_element_type=_F32)
                     for q_aug in q_augs]
            maxima = []
            for c, ((hh, s), sc) in enumerate(zip(chains, tiles)):
                if masked:
                    sc = jnp.where(key_iota + j * ATT_TK <= qry_iota + (q0 + s * ATT_TQ), sc, MASK_VALUE)
                s_ref[slot, c] = sc
                maxima.append(jnp.max(sc, axis=0, keepdims=True))
            return tuple(maxima)

        def consume(j, slot, maxima, state):
            out = []
            for c, ((hh, s), bm, (m, acc)) in enumerate(zip(chains, maxima, state)):
                m_new = jnp.maximum(m, bm)
                alpha = jnp.exp2(m - m_new)
                p = jnp.exp2(s_ref[slot, c] - m_new).astype(_BF16)
                vt = jnp.concatenate([vt_ref[j, hh * head_dim:(hh + 1) * head_dim, :], ones_rows], axis=0)
                out.append((m_new, alpha * acc + _dot(vt, p)))
            return tuple(out)

        def double_trip(t, carry):
            maxima, state = carry
            j = 2 * t
            prev = jnp.where(t == 0, n_full, j - 1)
            maxima_a = scores(j, 1, False)
            state = consume(prev, 0, maxima, state)
            maxima_b = scores(j + 1, 0, False)
            state = consume(j, 1, maxima_a, state)
            return maxima_b, state

        def odd_tail(carry):
            maxima, state = carry
            prev = jnp.where(n_full == 1, n_full, n_full - 2)
            maxima_a = scores(n_full - 1, 1, False)
            state = consume(prev, 0, maxima, state)
            return consume(n_full - 1, 1, maxima_a, state)

        def even_tail(carry):
            maxima, state = carry
            last = jnp.where(n_full == 0, n_full, n_full - 1)
            return consume(last, 0, maxima, state)

        init = (jnp.full((1, ATT_TQ), MASK_VALUE, _F32), jnp.zeros((head_dim + SUM_ROWS, ATT_TQ), _F32))
        carry = (scores(n_full, 0, True), (init,) * len(chains))
        carry = lax.fori_loop(0, n_full // 2, double_trip, carry)
        final = lax.cond(n_full % 2 == 1, odd_tail, even_tail, carry)
        for (hh, s), (_, acc) in zip(chains, final):
            cols = slice((g * group + s) * ATT_TQ, (g * group + s + 1) * ATT_TQ)
            out = acc[0:head_dim, :] / acc[head_dim:head_dim + 1, :]
            o_ref[hh * head_dim:(hh + 1) * head_dim, cols] = out.astype(_BF16)


def _attention(q, k, e, vt, *, n_streams, t_q, t_k, q_off, q_row0, k_row0, head_dim):
    hd = q.shape[1]
    n_pairs = hd // LANES
    tq_blk = min(t_q, 4 * ATT_TQ)
    assert t_q % tq_blk == 0 and tq_blk % ATT_TQ == 0 and t_k % ATT_TK == 0 and q_off % ATT_TQ == 0
    assert q_row0 % tq_blk == 0 and k_row0 % t_k == 0 and q_off + t_q <= t_k
    nq = t_q // tq_blk
    qb0, kb0, vb0 = q_row0 // tq_blk, k_row0 // t_k, k_row0 // ATT_TK
    nkb = t_k // ATT_TK
    assert vb0 % nkb == 0
    n_sub = tq_blk // ATT_TQ
    n_chains = (LANES // head_dim) * _attn_group(n_sub, q_off)
    return pl.pallas_call(
        functools.partial(_attn_body, n_sub=n_sub, q_off=q_off, head_dim=head_dim),
        grid=(n_streams, n_pairs, nq),
        in_specs=[
            pl.BlockSpec((tq_blk, LANES), lambda b, p, i: (qb0 + b * nq + i, p)),
            pl.BlockSpec((t_k, LANES), lambda b, p, i: (kb0 + b, p)),
            pl.BlockSpec((None, t_k, LANES), lambda b, p, i: (b, 0, 0)),
            pl.BlockSpec((nkb, LANES, ATT_TK), lambda b, p, i: (vb0 // nkb + b, p, 0)),
        ],
        out_specs=pl.BlockSpec((LANES, tq_blk), lambda b, p, i: (p, b * nq + i)),
        out_shape=jax.ShapeDtypeStruct((hd, n_streams * t_q), _BF16),
        scratch_shapes=[pltpu.VMEM((2, n_chains, ATT_TK, ATT_TQ), _F32)],
        compiler_params=_params("arbitrary", "arbitrary", "arbitrary"),
        name="forget_attention",
    )(q, k, e, vt)


def _oproj_body(x_ref, *refs, rows):
    ot_refs, wo_ref, y_ref = refs[:len(rows)], refs[len(rows)], refs[len(rows) + 1]
    o_t = _slab_read(ot_refs, rows)
    y_ref[...] = x_ref[...] + lax.dot_general(
        o_t, wo_ref[...], (((0,), (0,)), ((), ())), preferred_element_type=_F32)


def _o_project(x, o_ts, w_o):
    n, d = x.shape
    hd = w_o.shape[0]
    rows = tuple(o.shape[1] for o in o_ts)
    assert sum(rows) == n
    row = pl.BlockSpec((TOKEN_TILE, d), lambda i: (i, 0))
    return pl.pallas_call(
        functools.partial(_oproj_body, rows=rows),
        grid=(n // TOKEN_TILE,),
        in_specs=[row] + _slab_specs(rows, hd, transposed=True) + [_const_spec(w_o.shape)],
        out_specs=row,
        out_shape=jax.ShapeDtypeStruct((n, d), _F32),
        compiler_params=_params("arbitrary"),
        name="o_project",
    )(x, *o_ts, w_o)


def kernel(x_prompt, x_sample, cache_pool, cache_k, cache_v, cache_logf, ln_ffn1, ln_mix, ln_ffn2, w_ffn_in, w_ffn_out, w_pool, pool_scale, ln_kv, w_kv, w_fgate, b_fgate, w_q, w_o, ln_final):
    batch, seq, d = x_prompt.shape
    dec_batch, dec_seq, _ = x_sample.shape
    past_len, n_heads, head_dim = cache_k.shape[1:]
    hd = n_heads * head_dim
    depth = ln_ffn1.shape[0]
    n_a = w_pool.shape[0]
    n_prompt, n_sample = batch * seq, dec_batch * dec_seq
    rows = (n_prompt, n_sample)
    assert head_dim * 2 == LANES and seq % TOKEN_TILE == 0 and TOKEN_TILE == ATT_TK

    win = w_ffn_in.astype(_BF16)
    wout = w_ffn_out.astype(_BF16)
    wpool = w_pool.astype(_BF16)
    wkv = w_kv.astype(_BF16)
    wq = w_q.astype(_BF16)
    wo = w_o.astype(_BF16)
    wfg = jnp.pad(w_fgate, ((0, 0), (0, LANES - n_heads))).astype(_BF16)
    bfg = jnp.pad(b_fgate, (0, LANES - n_heads)).reshape(1, LANES)

    sample_q = -(-dec_seq // ATT_TQ) * ATT_TQ
    sample_keys = past_len + ATT_TK
    assert sample_q <= ATT_TK

    xs = [x_prompt.reshape(n_prompt, d), x_sample.reshape(n_sample, d)]
    x = None
    new_pool_prompt, new_pool_sample = [], []
    ks = vs = logf128 = k_bf = vt = e_prompt = e_sample = k_bf_sample = vt_sample = None
    for l in range(depth):
        last = l == depth - 1
        x = _ffn(xs if l == 0 else [x], ln_ffn1[l], win[l, 0], wout[l, 0])
        if l < n_a:
            hist_p = jnp.zeros((batch, POOL_HIST_ROWS, d), _F32)
            hist_s = jnp.pad(cache_pool[l], ((0, 0), (POOL_HIST_ROWS - POOL_HIST, 0), (0, 0)))
            xp, pool_p = _pool_mix(x, None, hist_p, ln_mix[l], wpool[l], pool_scale[l],
                                   n_streams=batch, t_len=seq, row0=0, pos0=0)
            x, pool_s = _pool_mix(x, xp, hist_s, ln_mix[l], wpool[l], pool_scale[l],
                                  n_streams=dec_batch, t_len=dec_seq, row0=n_prompt, pos0=past_len)
            new_pool_prompt.append(pool_p[:, POOL_HIST_ROWS - POOL_HIST:])
            new_pool_sample.append(pool_s[:, POOL_HIST_ROWS - POOL_HIST:])
        else:
            j = l - n_a
            q_bf = _q_project(x, ln_mix[l], wq[j], head_dim ** -0.5 * LOG2_E)
            o_prompt = _attention(q_bf, k_bf, e_prompt, vt, n_streams=batch, t_q=seq, t_k=seq, q_off=0,
                                  q_row0=0, k_row0=0, head_dim=head_dim)
            q_s = jnp.pad(q_bf[n_prompt:].reshape(dec_batch, dec_seq, hd),
                          ((0, 0), (0, sample_q - dec_seq), (0, 0))).reshape(dec_batch * sample_q, hd)
            o_sample = _attention(q_s, k_bf_sample, e_sample, vt_sample, n_streams=dec_batch, t_q=sample_q,
                                  t_k=sample_keys, q_off=past_len, q_row0=0, k_row0=0, head_dim=head_dim)
            o_sample = o_sample.reshape(hd, dec_batch, sample_q)[:, :, :dec_seq].reshape(hd, n_sample)
            x = _o_project(x, [o_prompt, o_sample], wo[j])
        x = _ffn([x], ln_ffn2[l], win[l, 1], wout[l, 1], ln_final if last else None, rows if last else None)
        if l == n_a - 1:
            ks, vs, logf128, k_bf, vt = _kv_project(x, rows, ln_kv, wkv, wfg, bfg, n_heads)
            e_prompt = _decay_features(logf128[:n_prompt].reshape(batch, seq, LANES), n_heads)
            logf_s = jnp.concatenate([
                jnp.pad(cache_logf, ((0, 0), (0, 0), (0, LANES - n_heads))),
                logf128[n_prompt:].reshape(dec_batch, dec_seq, LANES),
                jnp.zeros((dec_batch, ATT_TK - dec_seq, LANES), _F32)], axis=1)
            e_sample = _decay_features(logf_s, n_heads)
            k_bf_sample, vt_sample = _cache_keys(
                cache_k.reshape(dec_batch, past_len, hd), cache_v.reshape(dec_batch, past_len, hd),
                k_bf, vs[1], new_row0=n_prompt, dec_seq=dec_seq)

    y_prompt, y_sample = x
    logf = logf128[:, :n_heads]
    return (y_prompt.reshape(batch, seq, d), y_sample.reshape(dec_batch, dec_seq, d),
            jnp.stack(new_pool_prompt), jnp.stack(new_pool_sample),
            ks[0].reshape(batch, seq, n_heads, head_dim), vs[0].reshape(batch, seq, n_heads, head_dim),
            logf[:n_prompt].reshape(batch, seq, n_heads),
            ks[1].reshape(dec_batch, dec_seq, n_heads, head_dim),
            vs[1].reshape(dec_batch, dec_seq, n_heads, head_dim),
            logf[n_prompt:].reshape(dec_batch, dec_seq, n_heads))
```

```python
import functools

import jax
import jax.numpy as jnp
from jax import lax
from jax.experimental import pallas as pl
from jax.experimental.pallas import tpu as pltpu

RMS_EPS = 1e-6
POOL_WINDOWS = (2, 4, 8, 16)
POOL_HIST = max(POOL_WINDOWS) - 1

LANES = 128
TOKEN_TILE = 512
FFN_CHUNK = 256
POOL_PAD = 8
POOL_HIST_ROWS = 16
CUMSUM_CHUNK = 256
ATT_TQ = 256
ATT_TK = 512
SUM_ROWS = 16
MASK_VALUE = -1e30
LOG2_E = 1.4426950408889634
VMEM_LIMIT = 52 * 1024 * 1024

_BF16 = jnp.bfloat16
_F32 = jnp.float32


def _rms(x, g):
    ms = jnp.mean(x * x, axis=-1, keepdims=True)
    return x * lax.rsqrt(ms + RMS_EPS) * g


def _dot(a, b):
    return jnp.dot(a, b, preferred_element_type=_F32)


def _split3(x):
    hi = x.astype(_BF16)
    r = x - hi.astype(_F32)
    mid = r.astype(_BF16)
    lo = (r - mid.astype(_F32)).astype(_BF16)
    return hi, mid, lo


def _params(*sem):
    return pltpu.CompilerParams(dimension_semantics=sem, vmem_limit_bytes=VMEM_LIMIT)


def _const_spec(shape):
    zeros = (0,) * len(shape)
    return pl.BlockSpec(shape, lambda *_: zeros, pipeline_mode=pl.Buffered(1))


def _slab_blocks(rows):
    assert all(n % TOKEN_TILE == 0 for n in rows)
    return [n // TOKEN_TILE for n in rows]


def _slab_specs(rows, width, transposed=False):
    specs, start = [], 0
    for nb in _slab_blocks(rows):
        def index(i, start=start, nb=nb):
            blk = jnp.clip(i - start, 0, nb - 1)
            return (0, blk) if transposed else (blk, 0)
        shape = (width, TOKEN_TILE) if transposed else (TOKEN_TILE, width)
        specs.append(pl.BlockSpec(shape, index))
        start += nb
    return specs


def _slab_read(refs, rows):
    i = pl.program_id(0)
    blocks = _slab_blocks(rows)
    value, end = refs[-1][...], sum(blocks)
    for ref, nb in zip(refs[-2::-1], blocks[:0:-1]):
        end -= nb
        value = jnp.where(i < end, ref[...], value)
    return value


def _slab_write(refs, rows, value):
    i = pl.program_id(0)
    start = 0
    for ref, nb in zip(refs, _slab_blocks(rows)):
        @pl.when((i >= start) & (i < start + nb))
        def _(ref=ref):
            ref[...] = value
        start += nb


def _slab_shapes(rows, width, dtype):
    return [jax.ShapeDtypeStruct((n, width), dtype) for n in rows]


def _ffn_body(*refs, rows_in, rows_out, d_ff, final_norm):
    x_refs, refs = refs[:len(rows_in)], refs[len(rows_in):]
    g_ref, win_ref, wout_ref = refs[:3]
    refs = refs[3:]
    if final_norm:
        gf_ref, refs = refs[0], refs[1:]
    o_refs, acc_ref = refs[:len(rows_out)], refs[len(rows_out)]
    x = _slab_read(x_refs, rows_in)
    xn = _rms(x, g_ref[...]).astype(_BF16)
    for c in range(d_ff // FFN_CHUNK):
        lo, hi = c * FFN_CHUNK, (c + 1) * FFN_CHUNK
        g = _dot(xn, win_ref[:, lo:hi])
        u = _dot(xn, win_ref[:, d_ff + lo:d_ff + hi])
        a = (g * (1.0 / (1.0 + jnp.exp(-g))) * u).astype(_BF16)
        part = _dot(a, wout_ref[lo:hi, :])
        if c == 0:
            acc_ref[...] = part
        else:
            acc_ref[...] += part
    y = x + 0.5 * acc_ref[...]
    if final_norm:
        y = _rms(y, gf_ref[...])
    _slab_write(o_refs, rows_out, y)


def _ffn(xs, g, w_in, w_out, g_final=None, out_rows=None):
    d = xs[0].shape[1]
    d_ff = w_out.shape[0]
    assert d_ff % FFN_CHUNK == 0
    rows_in = tuple(x.shape[0] for x in xs)
    rows_out = tuple(out_rows) if out_rows else (sum(rows_in),)
    assert sum(rows_out) == sum(rows_in)
    in_specs = _slab_specs(rows_in, d) + [_const_spec((1, d)), _const_spec(w_in.shape), _const_spec(w_out.shape)]
    args = list(xs) + [g.reshape(1, d), w_in, w_out]
    if g_final is not None:
        in_specs.append(_const_spec((1, d)))
        args.append(g_final.reshape(1, d))
    out = pl.pallas_call(
        functools.partial(_ffn_body, rows_in=rows_in, rows_out=rows_out, d_ff=d_ff,
                          final_norm=g_final is not None),
        grid=(sum(_slab_blocks(rows_in)),),
        in_specs=in_specs,
        out_specs=_slab_specs(rows_out, d),
        out_shape=_slab_shapes(rows_out, d, _F32),
        scratch_shapes=[pltpu.VMEM((TOKEN_TILE, d), _F32)],
        compiler_params=_params("arbitrary"),
        name="ffn",
    )(*args)
    return out if out_rows else out[0]


def _pool_body(x_ref, hist_ref, g_ref, wp_ref, sc_ref, o_ref, pool_ref, buf, s2, s4, s8, *, tt, pos0):
    t = pl.program_id(1)
    d = x_ref.shape[-1]
    gw = d // len(POOL_WINDOWS)
    off = POOL_PAD + POOL_HIST_ROWS
    end = off + tt

    @pl.when(t == 0)
    def _():
        zeros = jnp.zeros((POOL_PAD, d), _F32)
        buf[0:POOL_PAD, :] = zeros
        s2[0:POOL_PAD, :] = zeros
        s4[0:POOL_PAD, :] = zeros
        buf[POOL_PAD:off, :] = hist_ref[...]

    @pl.when(t > 0)
    def _():
        buf[POOL_PAD:off, :] = buf[end - POOL_HIST_ROWS:end, :]

    x = x_ref[...]
    u = _rms(x, g_ref[...])
    buf[off:end, :] = u

    s2[POOL_PAD:end, :] = buf[POOL_PAD:end, :] + buf[POOL_PAD - 1:end - 1, :]
    s4[POOL_PAD:end, gw:] = s2[POOL_PAD:end, gw:] + s2[POOL_PAD - 2:end - 2, gw:]
    s8[POOL_PAD:end, 2 * gw:] = s4[POOL_PAD:end, 2 * gw:] + s4[POOL_PAD - 4:end - 4, 2 * gw:]
    sums = (
        s2[off:end, 0:gw],
        s4[off:end, gw:2 * gw],
        s8[off:end, 2 * gw:3 * gw],
        s8[off:end, 3 * gw:] + s8[off - 8:end - 8, 3 * gw:],
    )
    pos = pos0 + t * tt + lax.broadcasted_iota(jnp.int32, (tt, 1), 0)
    for gi, w in enumerate(POOL_WINDOWS):
        sl = slice(gi * gw, (gi + 1) * gw)
        inv_cnt = 1.0 / jnp.minimum(pos + 1, w).astype(_F32)
        diff = (sums[gi] * inv_cnt - u[:, sl]).astype(_BF16)
        y = _dot(diff, wp_ref[gi])
        o_ref[:, sl] = x[:, sl] + y * sc_ref[:, sl]

    @pl.when(t == pl.num_programs(1) - 1)
    def _():
        pool_ref[...] = buf[end - POOL_HIST_ROWS:end, :]


def _pool_mix(x_all, hist16, g, w_pool, scale, *, n_streams, t_len, row0, pos0):
    d = x_all.shape[1]
    tt = min(TOKEN_TILE, t_len)
    assert t_len % tt == 0 and row0 % tt == 0
    nt = t_len // tt
    blk0 = row0 // tt
    hist = pl.BlockSpec((None, POOL_HIST_ROWS, d), lambda b, t: (b, 0, 0))
    rows_buf = POOL_PAD + POOL_HIST_ROWS + tt
    return pl.pallas_call(
        functools.partial(_pool_body, tt=tt, pos0=pos0),
        grid=(n_streams, nt),
        in_specs=[pl.BlockSpec((tt, d), lambda b, t: (blk0 + b * nt + t, 0)), hist, _const_spec((1, d)),
                  _const_spec(w_pool.shape), _const_spec((1, d))],
        out_specs=[pl.BlockSpec((tt, d), lambda b, t: (b * nt + t, 0)), hist],
        out_shape=[jax.ShapeDtypeStruct((n_streams * t_len, d), _F32),
                   jax.ShapeDtypeStruct((n_streams, POOL_HIST_ROWS, d), _F32)],
        scratch_shapes=[pltpu.VMEM((rows_buf, d), _F32)] * 4,
        compiler_params=_params("arbitrary", "arbitrary"),
        name="pool_mix",
    )(x_all, hist16, g.reshape(1, d), w_pool, scale.reshape(1, d))


def _kv_body(x_ref, g_ref, wkv_ref, wf_ref, bf_ref, *out_refs, rows, hd, n_heads):
    ns = len(rows)
    k_refs, v_refs = out_refs[:ns], out_refs[ns:2 * ns]
    lf_ref, kb_ref, vt_ref = out_refs[2 * ns:]
    xn = _rms(x_ref[...], g_ref[...]).astype(_BF16)
    k = _dot(xn, wkv_ref[:, 0:hd])
    v = _dot(xn, wkv_ref[:, hd:2 * hd])
    _slab_write(k_refs, rows, k)
    _slab_write(v_refs, rows, v)
    kb_ref[...] = k.astype(_BF16)
    vt_ref[...] = v.T.astype(_BF16)
    z = _dot(xn, wf_ref[...]) + bf_ref[...]
    logf = jnp.minimum(z, 0.0) - jnp.log1p(jnp.exp(-jnp.abs(z)))
    lane = lax.broadcasted_iota(jnp.int32, logf.shape, 1)
    lf_ref[...] = jnp.where(lane < n_heads, logf, 0.0)


def _kv_project(x, rows, g, w_kv, w_fg_pad, b_fg_pad, n_heads):
    n, d = x.shape
    hd = w_kv.shape[1] // 2
    nb = n // TOKEN_TILE
    row = lambda width: pl.BlockSpec((TOKEN_TILE, width), lambda i: (i, 0))
    out = pl.pallas_call(
        functools.partial(_kv_body, rows=rows, hd=hd, n_heads=n_heads),
        grid=(nb,),
        in_specs=[row(d), _const_spec((1, d)), _const_spec(w_kv.shape), _const_spec(w_fg_pad.shape),
                  _const_spec((1, LANES))],
        out_specs=_slab_specs(rows, hd) + _slab_specs(rows, hd) + [
            row(LANES), row(hd), pl.BlockSpec((None, hd, TOKEN_TILE), lambda i: (i, 0, 0))],
        out_shape=_slab_shapes(rows, hd, _F32) + _slab_shapes(rows, hd, _F32) + [
            jax.ShapeDtypeStruct((n, LANES), _F32), jax.ShapeDtypeStruct((n, hd), _BF16),
            jax.ShapeDtypeStruct((nb, hd, TOKEN_TILE), _BF16)],
        compiler_params=_params("arbitrary"),
        name="kv_project",
    )(x, g.reshape(1, d), w_kv, w_fg_pad, b_fg_pad)
    ns = len(rows)
    return out[:ns], out[ns:2 * ns], out[2 * ns], out[2 * ns + 1], out[2 * ns + 2]


def _cache_body(ck_ref, cv_ref, kn_ref, vn_ref, k_ref, vt_ref, *, n_cache_blocks, dec_seq):
    j = pl.program_id(1)

    @pl.when(j < n_cache_blocks)
    def _():
        k_ref[...] = ck_ref[...].astype(_BF16)
        vt_ref[...] = cv_ref[...].T.astype(_BF16)

    @pl.when(j >= n_cache_blocks)
    def _():
        hd = k_ref.shape[1]
        pad = jnp.zeros((ATT_TK - dec_seq, hd), _F32)
        k_ref[...] = jnp.concatenate([kn_ref[...].astype(_F32), pad], axis=0).astype(_BF16)
        vt_ref[...] = jnp.concatenate([vn_ref[...], pad], axis=0).T.astype(_BF16)


def _cache_keys(cache_k, cache_v, k_new_bf, v_new, *, new_row0, dec_seq):
    s, past, hd = cache_k.shape
    assert past % ATT_TK == 0 and dec_seq % 16 == 0 and dec_seq <= ATT_TK and new_row0 % dec_seq == 0
    ncb = past // ATT_TK
    nkb = ncb + 1
    cache = pl.BlockSpec((None, ATT_TK, hd), lambda b, j: (b, jnp.minimum(j, ncb - 1), 0))
    return pl.pallas_call(
        functools.partial(_cache_body, n_cache_blocks=ncb, dec_seq=dec_seq),
        grid=(s, nkb),
        in_specs=[cache, cache,
                  pl.BlockSpec((dec_seq, hd), lambda b, j: (new_row0 // dec_seq + b, 0)),
                  pl.BlockSpec((dec_seq, hd), lambda b, j: (b, 0))],
        out_specs=[pl.BlockSpec((ATT_TK, hd), lambda b, j: (b * nkb + j, 0)),
                   pl.BlockSpec((None, hd, ATT_TK), lambda b, j: (b * nkb + j, 0, 0))],
        out_shape=[jax.ShapeDtypeStruct((s * nkb * ATT_TK, hd), _BF16),
                   jax.ShapeDtypeStruct((s * nkb, hd, ATT_TK), _BF16)],
        compiler_params=_params("arbitrary", "arbitrary"),
        name="cache_keys",
    )(cache_k, cache_v, k_new_bf, v_new)


def _decay_body(lf_ref, e_ref, c_ref, *, n_heads):
    n_chunks = lf_ref.shape[0] // CUMSUM_CHUNK
    r = lax.broadcasted_iota(jnp.int32, (CUMSUM_CHUNK, CUMSUM_CHUNK), 0)
    c = lax.broadcasted_iota(jnp.int32, (CUMSUM_CHUNK, CUMSUM_CHUNK), 1)
    tril = jnp.where(r >= c, 1.0, 0.0).astype(_BF16)
    sr = lax.broadcasted_iota(jnp.int32, (LANES, LANES), 0)
    sc = lax.broadcasted_iota(jnp.int32, (LANES, LANES), 1)
    place = [jnp.where((sc == 3 * sr + j) & (sr < n_heads), 1.0, 0.0).astype(_BF16) for j in range(3)]

    for i in range(n_chunks):
        rows = slice(i * CUMSUM_CHUNK, (i + 1) * CUMSUM_CHUNK)
        hi, mid, lo = _split3(lf_ref[rows, :])
        c_ref[rows, :] = _dot(tril, hi) + _dot(tril, mid) + _dot(tril, lo)
    carry = jnp.zeros((1, LANES), _F32)
    for i in range(n_chunks):
        rows = slice(i * CUMSUM_CHUNK, (i + 1) * CUMSUM_CHUNK)
        csum = c_ref[rows, :] + carry
        carry = carry + c_ref[(i + 1) * CUMSUM_CHUNK - 1:(i + 1) * CUMSUM_CHUNK, :]
        p0, p1, p2 = _split3(-LOG2_E * csum)
        feats = _dot(p0, place[0]) + _dot(p1, place[1]) + _dot(p2, place[2])
        e_ref[rows, :] = feats.astype(_BF16)


def _decay_features(logf, n_heads):
    b, t, _ = logf.shape
    assert t % CUMSUM_CHUNK == 0 and 3 * n_heads <= LANES
    blk = pl.BlockSpec((None, t, LANES), lambda i: (i, 0, 0))
    return pl.pallas_call(
        functools.partial(_decay_body, n_heads=n_heads),
        grid=(b,),
        in_specs=[blk],
        out_specs=blk,
        out_shape=jax.ShapeDtypeStruct((b, t, LANES), _BF16),
        scratch_shapes=[pltpu.VMEM((t, LANES), _F32)],
        compiler_params=_params("arbitrary"),
        name="decay_features",
    )(logf)


def _q_body(x_ref, g_ref, wq_ref, q_ref, *, scale):
    u = _rms(x_ref[...], g_ref[...]).astype(_BF16)
    q_ref[...] = (_dot(u, wq_ref[...]) * scale).astype(_BF16)


def _q_project(x, g, w_q, scale):
    n, d = x.shape
    hd = w_q.shape[1]
    return pl.pallas_call(
        functools.partial(_q_body, scale=scale),
        grid=(n // TOKEN_TILE,),
        in_specs=[pl.BlockSpec((TOKEN_TILE, d), lambda i: (i, 0)), _const_spec((1, d)), _const_spec(w_q.shape)],
        out_specs=pl.BlockSpec((TOKEN_TILE, hd), lambda i: (i, 0)),
        out_shape=jax.ShapeDtypeStruct((n, hd), _BF16),
        compiler_params=_params("arbitrary"),
        name="q_project",
    )(x, g.reshape(1, d), w_q)


def _attn_group(n_sub, q_off):
    return ATT_TK // ATT_TQ if (n_sub * ATT_TQ) % ATT_TK == 0 and q_off % ATT_TK == 0 else 1


def _attn_body(q_ref, k_ref, e_ref, vt_ref, o_ref, s_ref, *, n_sub, q_off, head_dim):
    pair = pl.program_id(1)
    qblk = pl.program_id(2)
    heads_per_block = LANES // head_dim
    group = _attn_group(n_sub, q_off)
    lane = lax.broadcasted_iota(jnp.int32, (ATT_TQ, LANES), 1)
    key_iota = lax.broadcasted_iota(jnp.int32, (ATT_TK, ATT_TQ), 0)
    qry_iota = lax.broadcasted_iota(jnp.int32, (ATT_TK, ATT_TQ), 1)
    chains = [(hh, s) for hh in range(heads_per_block) for s in range(group)]
    ones_rows = jnp.ones((SUM_ROWS, ATT_TK), _BF16)

    for g in range(n_sub // group):
        q0 = q_off + (qblk * n_sub + g * group) * ATT_TQ
        n_full = q0 // ATT_TK
        q_augs = []
        for hh, s in chains:
            head = pair * heads_per_block + hh
            bias_sel = jnp.where((lane >= 3 * head) & (lane < 3 * head + 3), 1.0, 0.0).astype(_BF16)
            head_lanes = (lane >= hh * head_dim) & (lane < (hh + 1) * head_dim)
            q = q_ref[(g * group + s) * ATT_TQ:(g * group + s + 1) * ATT_TQ, :]
            q_augs.append(jnp.concatenate([jnp.where(head_lanes, q, jnp.zeros_like(q)), bias_sel], axis=1))

        def scores(j, slot, masked):
            rows = pl.ds(pl.multiple_of(j * ATT_TK, ATT_TK), ATT_TK)
            k_aug = jnp.concatenate([k_ref[rows, :], e_ref[rows, :]], axis=1)
            tiles = [lax.dot_general(k_aug, q_aug, (((1,), (1,)), ((), ())), preferred_element_type=_F32)
                     for q_aug in q_augs]
            maxima = []
            for c, ((hh, s), sc) in enumerate(zip(chains, tiles)):
                if masked:
                    sc = jnp.where(key_iota + j * ATT_TK <= qry_iota + (q0 + s * ATT_TQ), sc, MASK_VALUE)
                s_ref[slot, c] = sc
                maxima.append(jnp.max(sc, axis=0, keepdims=True))
            return tuple(maxima)

        def consume(j, slot, maxima, state):
            out = []
            for c, ((hh, s), bm, (m, acc)) in enumerate(zip(chains, maxima, state)):
                m_new = jnp.maximum(m, bm)
                alpha = jnp.exp2(m - m_new)
                p = jnp.exp2(s_ref[slot, c] - m_new).astype(_BF16)
                vt = jnp.concatenate([vt_ref[j, hh * head_dim:(hh + 1) * head_dim, :], ones_rows], axis=0)
                out.append((m_new, alpha * acc + _dot(vt, p)))
            return tuple(out)

        def double_trip(t, carry):
            maxima, state = carry
            j = 2 * t
            prev = jnp.where(t == 0, n_full, j - 1)
            maxima_a = scores(j, 1, False)
            state = consume(prev, 0, maxima, state)
            maxima_b = scores(j + 1, 0, False)
            state = consume(j, 1, maxima_a, state)
            return maxima_b, state

        def odd_tail(carry):
            maxima, state = carry
            prev = jnp.where(n_full == 1, n_full, n_full - 2)
            maxima_a = scores(n_full - 1, 1, False)
            state = consume(prev, 0, maxima, state)
            return consume(n_full - 1, 1, maxima_a, state)

        def even_tail(carry):
            maxima, state = carry
            last = jnp.where(n_full == 0, n_full, n_full - 1)
            return consume(last, 0, maxima, state)

        init = (jnp.full((1, ATT_TQ), MASK_VALUE, _F32), jnp.zeros((head_dim + SUM_ROWS, ATT_TQ), _F32))
        carry = (scores(n_full, 0, True), (init,) * len(chains))
        carry = lax.fori_loop(0, n_full // 2, double_trip, carry)
        final = lax.cond(n_full % 2 == 1, odd_tail, even_tail, carry)
        for (hh, s), (_, acc) in zip(chains, final):
            cols = slice((g * group + s) * ATT_TQ, (g * group + s + 1) * ATT_TQ)
            out = acc[0:head_dim, :] / acc[head_dim:head_dim + 1, :]
            o_ref[hh * head_dim:(hh + 1) * head_dim, cols] = out.astype(_BF16)


def _attention(q, k, e, vt, *, n_streams, t_q, t_k, q_off, q_row0, k_row0, head_dim):
    hd = q.shape[1]
    n_pairs = hd // LANES
    tq_blk = min(t_q, 4 * ATT_TQ)
    assert t_q % tq_blk == 0 and tq_blk % ATT_TQ == 0 and t_k % ATT_TK == 0 and q_off % ATT_TQ == 0
    assert q_row0 % tq_blk == 0 and k_row0 % t_k == 0 and q_off + t_q <= t_k
    nq = t_q // tq_blk
    qb0, kb0, vb0 = q_row0 // tq_blk, k_row0 // t_k, k_row0 // ATT_TK
    nkb = t_k // ATT_TK
    assert vb0 % nkb == 0
    n_sub = tq_blk // ATT_TQ
    n_chains = (LANES // head_dim) * _attn_group(n_sub, q_off)
    return pl.pallas_call(
        functools.partial(_attn_body, n_sub=n_sub, q_off=q_off, head_dim=head_dim),
        grid=(n_streams, n_pairs, nq),
        in_specs=[
            pl.BlockSpec((tq_blk, LANES), lambda b, p, i: (qb0 + b * nq + i, p)),
            pl.BlockSpec((t_k, LANES), lambda b, p, i: (kb0 + b, p)),
            pl.BlockSpec((None, t_k, LANES), lambda b, p, i: (b, 0, 0)),
            pl.BlockSpec((nkb, LANES, ATT_TK), lambda b, p, i: (vb0 // nkb + b, p, 0)),
        ],
        out_specs=pl.BlockSpec((LANES, tq_blk), lambda b, p, i: (p, b * nq + i)),
        out_shape=jax.ShapeDtypeStruct((hd, n_streams * t_q), _BF16),
        scratch_shapes=[pltpu.VMEM((2, n_chains, ATT_TK, ATT_TQ), _F32)],
        compiler_params=_params("arbitrary", "arbitrary", "arbitrary"),
        name="forget_attention",
    )(q, k, e, vt)


def _oproj_body(x_ref, *refs, rows):
    ot_refs, wo_ref, y_ref = refs[:len(rows)], refs[len(rows)], refs[len(rows) + 1]
    o_t = _slab_read(ot_refs, rows)
    y_ref[...] = x_ref[...] + lax.dot_general(
        o_t, wo_ref[...], (((0,), (0,)), ((), ())), preferred_element_type=_F32)


def _o_project(x, o_ts, w_o):
    n, d = x.shape
    hd = w_o.shape[0]
    rows = tuple(o.shape[1] for o in o_ts)
    assert sum(rows) == n
    row = pl.BlockSpec((TOKEN_TILE, d), lambda i: (i, 0))
    return pl.pallas_call(
        functools.partial(_oproj_body, rows=rows),
        grid=(n // TOKEN_TILE,),
        in_specs=[row] + _slab_specs(rows, hd, transposed=True) + [_const_spec(w_o.shape)],
        out_specs=row,
        out_shape=jax.ShapeDtypeStruct((n, d), _F32),
        compiler_params=_params("arbitrary"),
        name="o_project",
    )(x, *o_ts, w_o)


def kernel(x_prompt, x_sample, cache_pool, cache_k, cache_v, cache_logf, ln_ffn1, ln_mix, ln_ffn2, w_ffn_in, w_ffn_out, w_pool, pool_scale, ln_kv, w_kv, w_fgate, b_fgate, w_q, w_o, ln_final):
    batch, seq, d = x_prompt.shape
    dec_batch, dec_seq, _ = x_sample.shape
    past_len, n_heads, head_dim = cache_k.shape[1:]
    hd = n_heads * head_dim
    depth = ln_ffn1.shape[0]
    n_a = w_pool.shape[0]
    n_prompt, n_sample = batch * seq, dec_batch * dec_seq
    rows = (n_prompt, n_sample)
    assert head_dim * 2 == LANES and seq % TOKEN_TILE == 0 and TOKEN_TILE == ATT_TK

    win = w_ffn_in.astype(_BF16)
    wout = w_ffn_out.astype(_BF16)
    wpool = w_pool.astype(_BF16)
    wkv = w_kv.astype(_BF16)
    wq = w_q.astype(_BF16)
    wo = w_o.astype(_BF16)
    wfg = jnp.pad(w_fgate, ((0, 0), (0, LANES - n_heads))).astype(_BF16)
    bfg = jnp.pad(b_fgate, (0, LANES - n_heads)).reshape(1, LANES)

    sample_q = -(-dec_seq // ATT_TQ) * ATT_TQ
    sample_keys = past_len + ATT_TK
    assert sample_q <= ATT_TK

    slabs = [x_prompt.reshape(n_prompt, d), x_sample.reshape(n_sample, d)]
    new_pool_prompt, new_pool_sample = [], []
    ks = vs = logf128 = k_bf = vt = e_prompt = e_sample = k_bf_sample = vt_sample = None
    for l in range(depth):
        last = l == depth - 1
        x = _ffn(slabs, ln_ffn1[l], win[l, 0], wout[l, 0])
        if l < n_a:
            hist_p = jnp.zeros((batch, POOL_HIST_ROWS, d), _F32)
            hist_s = jnp.pad(cache_pool[l], ((0, 0), (POOL_HIST_ROWS - POOL_HIST, 0), (0, 0)))
            xp, pool_p = _pool_mix(x, hist_p, ln_mix[l], wpool[l], pool_scale[l],
                                   n_streams=batch, t_len=seq, row0=0, pos0=0)
            xs, pool_s = _pool_mix(x, hist_s, ln_mix[l], wpool[l], pool_scale[l],
                                   n_streams=dec_batch, t_len=dec_seq, row0=n_prompt, pos0=past_len)
            slabs = [xp, xs]
            new_pool_prompt.append(pool_p[:, POOL_HIST_ROWS - POOL_HIST:])
            new_pool_sample.append(pool_s[:, POOL_HIST_ROWS - POOL_HIST:])
        else:
            j = l - n_a
            q_bf = _q_project(x, ln_mix[l], wq[j], head_dim ** -0.5 * LOG2_E)
            o_prompt = _attention(q_bf, k_bf, e_prompt, vt, n_streams=batch, t_q=seq, t_k=seq, q_off=0,
                                  q_row0=0, k_row0=0, head_dim=head_dim)
            q_s = jnp.pad(q_bf[n_prompt:].reshape(dec_batch, dec_seq, hd),
                          ((0, 0), (0, sample_q - dec_seq), (0, 0))).reshape(dec_batch * sample_q, hd)
            o_sample = _attention(q_s, k_bf_sample, e_sample, vt_sample, n_streams=dec_batch, t_q=sample_q,
                                  t_k=sample_keys, q_off=past_len, q_row0=0, k_row0=0, head_dim=head_dim)
            o_sample = o_sample.reshape(hd, dec_batch, sample_q)[:, :, :dec_seq].reshape(hd, n_sample)
            slabs = [_o_project(x, [o_prompt, o_sample], wo[j])]
        x = _ffn(slabs, ln_ffn2[l], win[l, 1], wout[l, 1], ln_final if last else None, rows if last else None)
        slabs = x if last else [x]
        if l == n_a - 1:
            ks, vs, logf128, k_bf, vt = _kv_project(x, rows, ln_kv, wkv, wfg, bfg, n_heads)
            e_prompt = _decay_features(logf128[:n_prompt].reshape(batch, seq, LANES), n_heads)
            logf_s = jnp.concatenate([
                jnp.pad(cache_logf, ((0, 0), (0, 0), (0, LANES - n_heads))),
                logf128[n_prompt:].reshape(dec_batch, dec_seq, LANES),
                jnp.zeros((dec_batch, ATT_TK - dec_seq, LANES), _F32)], axis=1)
            e_sample = _decay_features(logf_s, n_heads)
            k_bf_sample, vt_sample = _cache_keys(
                cache_k.reshape(dec_batch, past_len, hd), cache_v.reshape(dec_batch, past_len, hd),
                k_bf, vs[1], new_row0=n_prompt, dec_seq=dec_seq)

    y_prompt, y_sample = slabs
    logf = logf128[:, :n_heads]
    return (y_prompt.reshape(batch, seq, d), y_sample.reshape(dec_batch, dec_seq, d),
            jnp.stack(new_pool_prompt), jnp.stack(new_pool_sample),
            ks[0].reshape(batch, seq, n_heads, head_dim), vs[0].reshape(batch, seq, n_heads, head_dim),
            logf[:n_prompt].reshape(batch, seq, n_heads),
            ks[1].reshape(dec_batch, dec_seq, n_heads, head_dim),
            vs[1].reshape(dec_batch, dec_seq, n_heads, head_dim),
            logf[n_prompt:].reshape(dec_batch, dec_seq, n_heads))
```

```python
import functools

import jax
import jax.numpy as jnp
from jax import lax
from jax.experimental import pallas as pl
from jax.experimental.pallas import tpu as pltpu

RMS_EPS = 1e-6
POOL_WINDOWS = (2, 4, 8, 16)
POOL_HIST = max(POOL_WINDOWS) - 1

LANES = 128
TOKEN_TILE = 512
FFN_CHUNK = 256
POOL_PAD = 8
POOL_HIST_ROWS = 16
CUMSUM_CHUNK = 256
ATT_TQ = 256
ATT_TK = 512
SUM_ROWS = 16
MASK_VALUE = -1e30
LOG2_E = 1.4426950408889634
VMEM_LIMIT = 52 * 1024 * 1024

_BF16 = jnp.bfloat16
_F32 = jnp.float32


def _rms(x, g):
    ms = jnp.mean(x * x, axis=-1, keepdims=True)
    return x * lax.rsqrt(ms + RMS_EPS) * g


def _dot(a, b):
    return jnp.dot(a, b, preferred_element_type=_F32)


def _split3(x):
    hi = x.astype(_BF16)
    r = x - hi.astype(_F32)
    mid = r.astype(_BF16)
    lo = (r - mid.astype(_F32)).astype(_BF16)
    return hi, mid, lo


def _params(*sem):
    return pltpu.CompilerParams(dimension_semantics=sem, vmem_limit_bytes=VMEM_LIMIT)


def _const_spec(shape):
    zeros = (0,) * len(shape)
    return pl.BlockSpec(shape, lambda *_: zeros, pipeline_mode=pl.Buffered(1))


def _slab_blocks(rows):
    assert all(n % TOKEN_TILE == 0 for n in rows)
    return [n // TOKEN_TILE for n in rows]


def _slab_specs(rows, width, transposed=False):
    tail = width if isinstance(width, tuple) else (width,)
    specs, start = [], 0
    for nb in _slab_blocks(rows):
        def index(i, start=start, nb=nb):
            blk = jnp.clip(i - start, 0, nb - 1)
            return (0, blk) if transposed else (blk,) + (0,) * len(tail)
        shape = tail + (TOKEN_TILE,) if transposed else (TOKEN_TILE,) + tail
        mode = dict(pipeline_mode=pl.Buffered(1)) if nb == 1 else {}
        specs.append(pl.BlockSpec(shape, index, **mode))
        start += nb
    return specs


def _slab_read(refs, rows):
    i = pl.program_id(0)
    blocks = _slab_blocks(rows)
    value, end = refs[-1][...], sum(blocks)
    for ref, nb in zip(refs[-2::-1], blocks[:0:-1]):
        end -= nb
        value = jnp.where(i < end, ref[...], value)
    return value


def _slab_write(refs, rows, value):
    i = pl.program_id(0)
    start = 0
    for ref, nb in zip(refs, _slab_blocks(rows)):
        @pl.when((i >= start) & (i < start + nb))
        def _(ref=ref):
            ref[...] = value
        start += nb


def _slab_shapes(rows, width, dtype):
    tail = width if isinstance(width, tuple) else (width,)
    return [jax.ShapeDtypeStruct((n,) + tail, dtype) for n in rows]


def _ffn_body(*refs, rows_in, rows_out, d_ff, final_norm):
    x_refs, refs = refs[:len(rows_in)], refs[len(rows_in):]
    g_ref, win_ref, wout_ref = refs[:3]
    refs = refs[3:]
    if final_norm:
        gf_ref, refs = refs[0], refs[1:]
    o_refs, acc_ref = refs[:len(rows_out)], refs[len(rows_out)]
    x = _slab_read(x_refs, rows_in)
    xn = _rms(x, g_ref[...]).astype(_BF16)
    for c in range(d_ff // FFN_CHUNK):
        lo, hi = c * FFN_CHUNK, (c + 1) * FFN_CHUNK
        g = _dot(xn, win_ref[:, lo:hi])
        u = _dot(xn, win_ref[:, d_ff + lo:d_ff + hi])
        a = (g * (1.0 / (1.0 + jnp.exp(-g))) * u).astype(_BF16)
        part = _dot(a, wout_ref[lo:hi, :])
        if c == 0:
            acc_ref[...] = part
        else:
            acc_ref[...] += part
    y = x + 0.5 * acc_ref[...]
    if final_norm:
        y = _rms(y, gf_ref[...])
    _slab_write(o_refs, rows_out, y)


def _ffn(xs, g, w_in, w_out, g_final=None, out_rows=None):
    d = xs[0].shape[1]
    d_ff = w_out.shape[0]
    assert d_ff % FFN_CHUNK == 0
    rows_in = tuple(x.shape[0] for x in xs)
    rows_out = tuple(out_rows) if out_rows else (sum(rows_in),)
    assert sum(rows_out) == sum(rows_in)
    in_specs = _slab_specs(rows_in, d) + [_const_spec((1, d)), _const_spec(w_in.shape), _const_spec(w_out.shape)]
    args = list(xs) + [g.reshape(1, d), w_in, w_out]
    if g_final is not None:
        in_specs.append(_const_spec((1, d)))
        args.append(g_final.reshape(1, d))
    out = pl.pallas_call(
        functools.partial(_ffn_body, rows_in=rows_in, rows_out=rows_out, d_ff=d_ff,
                          final_norm=g_final is not None),
        grid=(sum(_slab_blocks(rows_in)),),
        in_specs=in_specs,
        out_specs=_slab_specs(rows_out, d),
        out_shape=_slab_shapes(rows_out, d, _F32),
        scratch_shapes=[pltpu.VMEM((TOKEN_TILE, d), _F32)],
        compiler_params=_params("arbitrary"),
        name="ffn",
    )(*args)
    return out if out_rows else out[0]


def _pool_body(x_ref, hist_ref, g_ref, wp_ref, sc_ref, o_ref, pool_ref, buf, s2, s4, s8, *, tt, pos0):
    t = pl.program_id(1)
    d = x_ref.shape[-1]
    gw = d // len(POOL_WINDOWS)
    off = POOL_PAD + POOL_HIST_ROWS
    end = off + tt

    @pl.when(t == 0)
    def _():
        zeros = jnp.zeros((POOL_PAD, d), _F32)
        buf[0:POOL_PAD, :] = zeros
        s2[0:POOL_PAD, :] = zeros
        s4[0:POOL_PAD, :] = zeros
        buf[POOL_PAD:off, :] = hist_ref[...]

    @pl.when(t > 0)
    def _():
        buf[POOL_PAD:off, :] = buf[end - POOL_HIST_ROWS:end, :]

    x = x_ref[...]
    u = _rms(x, g_ref[...])
    buf[off:end, :] = u

    s2[POOL_PAD:end, :] = buf[POOL_PAD:end, :] + buf[POOL_PAD - 1:end - 1, :]
    s4[POOL_PAD:end, gw:] = s2[POOL_PAD:end, gw:] + s2[POOL_PAD - 2:end - 2, gw:]
    s8[POOL_PAD:end, 2 * gw:] = s4[POOL_PAD:end, 2 * gw:] + s4[POOL_PAD - 4:end - 4, 2 * gw:]
    sums = (
        s2[off:end, 0:gw],
        s4[off:end, gw:2 * gw],
        s8[off:end, 2 * gw:3 * gw],
        s8[off:end, 3 * gw:] + s8[off - 8:end - 8, 3 * gw:],
    )
    pos = pos0 + t * tt + lax.broadcasted_iota(jnp.int32, (tt, 1), 0)
    for gi, w in enumerate(POOL_WINDOWS):
        sl = slice(gi * gw, (gi + 1) * gw)
        inv_cnt = 1.0 / jnp.minimum(pos + 1, w).astype(_F32)
        diff = (sums[gi] * inv_cnt - u[:, sl]).astype(_BF16)
        y = _dot(diff, wp_ref[gi])
        o_ref[:, sl] = x[:, sl] + y * sc_ref[:, sl]

    @pl.when(t == pl.num_programs(1) - 1)
    def _():
        pool_ref[...] = buf[end - POOL_HIST_ROWS:end, :]


def _pool_mix(x_all, hist16, g, w_pool, scale, *, n_streams, t_len, row0, pos0):
    d = x_all.shape[1]
    tt = min(TOKEN_TILE, t_len)
    assert t_len % tt == 0 and row0 % tt == 0
    nt = t_len // tt
    blk0 = row0 // tt
    hist = pl.BlockSpec((None, POOL_HIST_ROWS, d), lambda b, t: (b, 0, 0))
    rows_buf = POOL_PAD + POOL_HIST_ROWS + tt
    return pl.pallas_call(
        functools.partial(_pool_body, tt=tt, pos0=pos0),
        grid=(n_streams, nt),
        in_specs=[pl.BlockSpec((tt, d), lambda b, t: (blk0 + b * nt + t, 0)), hist, _const_spec((1, d)),
                  _const_spec(w_pool.shape), _const_spec((1, d))],
        out_specs=[pl.BlockSpec((tt, d), lambda b, t: (b * nt + t, 0)), hist],
        out_shape=[jax.ShapeDtypeStruct((n_streams * t_len, d), _F32),
                   jax.ShapeDtypeStruct((n_streams, POOL_HIST_ROWS, d), _F32)],
        scratch_shapes=[pltpu.VMEM((rows_buf, d), _F32)] * 4,
        compiler_params=_params("arbitrary", "arbitrary"),
        name="pool_mix",
    )(x_all, hist16, g.reshape(1, d), w_pool, scale.reshape(1, d))


def _kv_body(x_ref, g_ref, wkv_ref, wf_ref, bf_ref, *out_refs, rows, hd, n_heads):
    ns = len(rows)
    k_refs, v_refs = out_refs[:ns], out_refs[ns:2 * ns]
    lf_ref, kb_ref, vt_ref = out_refs[2 * ns:]
    xn = _rms(x_ref[...], g_ref[...]).astype(_BF16)
    k = _dot(xn, wkv_ref[:, 0:hd])
    v = _dot(xn, wkv_ref[:, hd:2 * hd])
    _slab_write(k_refs, rows, k.reshape(k.shape[0], n_heads, hd // n_heads))
    _slab_write(v_refs, rows, v.reshape(v.shape[0], n_heads, hd // n_heads))
    kb_ref[...] = k.astype(_BF16)
    vt_ref[...] = v.T.astype(_BF16)
    z = _dot(xn, wf_ref[...]) + bf_ref[...]
    logf = jnp.minimum(z, 0.0) - jnp.log1p(jnp.exp(-jnp.abs(z)))
    lane = lax.broadcasted_iota(jnp.int32, logf.shape, 1)
    lf_ref[...] = jnp.where(lane < n_heads, logf, 0.0)


def _kv_project(x, rows, g, w_kv, w_fg_pad, b_fg_pad, n_heads):
    n, d = x.shape
    hd = w_kv.shape[1] // 2
    heads = (n_heads, hd // n_heads)
    nb = n // TOKEN_TILE
    row = lambda width: pl.BlockSpec((TOKEN_TILE, width), lambda i: (i, 0))
    out = pl.pallas_call(
        functools.partial(_kv_body, rows=rows, hd=hd, n_heads=n_heads),
        grid=(nb,),
        in_specs=[row(d), _const_spec((1, d)), _const_spec(w_kv.shape), _const_spec(w_fg_pad.shape),
                  _const_spec((1, LANES))],
        out_specs=_slab_specs(rows, heads) + _slab_specs(rows, heads) + [
            row(LANES), row(hd), pl.BlockSpec((None, hd, TOKEN_TILE), lambda i: (i, 0, 0))],
        out_shape=_slab_shapes(rows, heads, _F32) + _slab_shapes(rows, heads, _F32) + [
            jax.ShapeDtypeStruct((n, LANES), _F32), jax.ShapeDtypeStruct((n, hd), _BF16),
            jax.ShapeDtypeStruct((nb, hd, TOKEN_TILE), _BF16)],
        compiler_params=_params("arbitrary"),
        name="kv_project",
    )(x, g.reshape(1, d), w_kv, w_fg_pad, b_fg_pad)
    ns = len(rows)
    return out[:ns], out[ns:2 * ns], out[2 * ns], out[2 * ns + 1], out[2 * ns + 2]


def _merge_heads(x):
    return x.reshape(x.shape[0], x.shape[1] * x.shape[2])


def _cache_body(ck_ref, cv_ref, kn_ref, vn_ref, k_ref, vt_ref, *, n_cache_blocks, dec_seq):
    j = pl.program_id(1)

    @pl.when(j < n_cache_blocks)
    def _():
        k_ref[...] = _merge_heads(ck_ref[...]).astype(_BF16)
        vt_ref[...] = _merge_heads(cv_ref[...]).T.astype(_BF16)

    @pl.when(j >= n_cache_blocks)
    def _():
        hd = k_ref.shape[1]
        pad = jnp.zeros((ATT_TK - dec_seq, hd), _F32)
        k_ref[...] = jnp.concatenate([kn_ref[...].astype(_F32), pad], axis=0).astype(_BF16)
        vt_ref[...] = jnp.concatenate([_merge_heads(vn_ref[...]), pad], axis=0).T.astype(_BF16)


def _cache_keys(cache_k, cache_v, k_new_bf, v_new, *, new_row0, dec_seq):
    s, past, n_heads, head_dim = cache_k.shape
    hd = n_heads * head_dim
    assert past % ATT_TK == 0 and dec_seq % 16 == 0 and dec_seq <= ATT_TK and new_row0 % dec_seq == 0
    ncb = past // ATT_TK
    nkb = ncb + 1
    cache = pl.BlockSpec((None, ATT_TK, n_heads, head_dim), lambda b, j: (b, jnp.minimum(j, ncb - 1), 0, 0))
    return pl.pallas_call(
        functools.partial(_cache_body, n_cache_blocks=ncb, dec_seq=dec_seq),
        grid=(s, nkb),
        in_specs=[cache, cache,
                  pl.BlockSpec((dec_seq, hd), lambda b, j: (new_row0 // dec_seq + b, 0)),
                  pl.BlockSpec((dec_seq, n_heads, head_dim), lambda b, j: (b, 0, 0))],
        out_specs=[pl.BlockSpec((ATT_TK, hd), lambda b, j: (b * nkb + j, 0)),
                   pl.BlockSpec((None, hd, ATT_TK), lambda b, j: (b * nkb + j, 0, 0))],
        out_shape=[jax.ShapeDtypeStruct((s * nkb * ATT_TK, hd), _BF16),
                   jax.ShapeDtypeStruct((s * nkb, hd, ATT_TK), _BF16)],
        compiler_params=_params("arbitrary", "arbitrary"),
        name="cache_keys",
    )(cache_k, cache_v, k_new_bf, v_new)


def _decay_body(lf_ref, e_ref, c_ref, *, n_heads):
    n_chunks = lf_ref.shape[0] // CUMSUM_CHUNK
    r = lax.broadcasted_iota(jnp.int32, (CUMSUM_CHUNK, CUMSUM_CHUNK), 0)
    c = lax.broadcasted_iota(jnp.int32, (CUMSUM_CHUNK, CUMSUM_CHUNK), 1)
    tril = jnp.where(r >= c, 1.0, 0.0).astype(_BF16)
    sr = lax.broadcasted_iota(jnp.int32, (LANES, LANES), 0)
    sc = lax.broadcasted_iota(jnp.int32, (LANES, LANES), 1)
    place = [jnp.where((sc == 3 * sr + j) & (sr < n_heads), 1.0, 0.0).astype(_BF16) for j in range(3)]

    for i in range(n_chunks):
        rows = slice(i * CUMSUM_CHUNK, (i + 1) * CUMSUM_CHUNK)
        hi, mid, lo = _split3(lf_ref[rows, :])
        c_ref[rows, :] = _dot(tril, hi) + _dot(tril, mid) + _dot(tril, lo)
    carry = jnp.zeros((1, LANES), _F32)
    for i in range(n_chunks):
        rows = slice(i * CUMSUM_CHUNK, (i + 1) * CUMSUM_CHUNK)
        csum = c_ref[rows, :] + carry
        carry = carry + c_ref[(i + 1) * CUMSUM_CHUNK - 1:(i + 1) * CUMSUM_CHUNK, :]
        p0, p1, p2 = _split3(-LOG2_E * csum)
        feats = _dot(p0, place[0]) + _dot(p1, place[1]) + _dot(p2, place[2])
        e_ref[rows, :] = feats.astype(_BF16)


def _decay_features(logf, n_heads):
    b, t, _ = logf.shape
    assert t % CUMSUM_CHUNK == 0 and 3 * n_heads <= LANES
    blk = pl.BlockSpec((None, t, LANES), lambda i: (i, 0, 0))
    return pl.pallas_call(
        functools.partial(_decay_body, n_heads=n_heads),
        grid=(b,),
        in_specs=[blk],
        out_specs=blk,
        out_shape=jax.ShapeDtypeStruct((b, t, LANES), _BF16),
        scratch_shapes=[pltpu.VMEM((t, LANES), _F32)],
        compiler_params=_params("arbitrary"),
        name="decay_features",
    )(logf)


def _q_body(x_ref, g_ref, wq_ref, q_ref, *, scale):
    u = _rms(x_ref[...], g_ref[...]).astype(_BF16)
    q_ref[...] = (_dot(u, wq_ref[...]) * scale).astype(_BF16)


def _q_project(x, g, w_q, scale):
    n, d = x.shape
    hd = w_q.shape[1]
    return pl.pallas_call(
        functools.partial(_q_body, scale=scale),
        grid=(n // TOKEN_TILE,),
        in_specs=[pl.BlockSpec((TOKEN_TILE, d), lambda i: (i, 0)), _const_spec((1, d)), _const_spec(w_q.shape)],
        out_specs=pl.BlockSpec((TOKEN_TILE, hd), lambda i: (i, 0)),
        out_shape=jax.ShapeDtypeStruct((n, hd), _BF16),
        compiler_params=_params("arbitrary"),
        name="q_project",
    )(x, g.reshape(1, d), w_q)


def _attn_group(n_sub, q_off):
    return ATT_TK // ATT_TQ if (n_sub * ATT_TQ) % ATT_TK == 0 and q_off % ATT_TK == 0 else 1


def _attn_body(q_ref, k_ref, e_ref, vt_ref, o_ref, s_ref, *, n_sub, q_off, head_dim):
    pair = pl.program_id(1)
    qblk = pl.program_id(2)
    heads_per_block = LANES // head_dim
    group = _attn_group(n_sub, q_off)
    lane = lax.broadcasted_iota(jnp.int32, (ATT_TQ, LANES), 1)
    feat = lax.broadcasted_iota(jnp.int32, (LANES, ATT_TQ), 0)
    key_iota = lax.broadcasted_iota(jnp.int32, (ATT_TK, ATT_TQ), 0)
    qry_iota = lax.broadcasted_iota(jnp.int32, (ATT_TK, ATT_TQ), 1)
    chains = [(hh, s) for hh in range(heads_per_block) for s in range(group)]
    ones_rows = jnp.ones((SUM_ROWS, ATT_TK), _BF16)

    for g in range(n_sub // group):
        q0 = q_off + (qblk * n_sub + g * group) * ATT_TQ
        n_full = q0 // ATT_TK
        q_augs = []
        for hh, s in chains:
            head = pair * heads_per_block + hh
            bias_sel = jnp.where((feat >= 3 * head) & (feat < 3 * head + 3), 1.0, 0.0).astype(_BF16)
            head_lanes = (lane >= hh * head_dim) & (lane < (hh + 1) * head_dim)
            q = q_ref[(g * group + s) * ATT_TQ:(g * group + s + 1) * ATT_TQ, :].astype(_F32)
            q_t = jnp.where(head_lanes, q, 0.0).T.astype(_BF16)
            q_augs.append(jnp.concatenate([q_t, bias_sel], axis=0))

        def scores(j, slot, masked):
            rows = pl.ds(pl.multiple_of(j * ATT_TK, ATT_TK), ATT_TK)
            k_aug = jnp.concatenate([k_ref[rows, :], e_ref[rows, :]], axis=1)
            tiles = [_dot(k_aug, q_aug) for q_aug in q_augs]
            maxima = []
            for c, ((hh, s), sc) in enumerate(zip(chains, tiles)):
                if masked:
                    sc = jnp.where(key_iota + j * ATT_TK <= qry_iota + (q0 + s * ATT_TQ), sc, MASK_VALUE)
                s_ref[slot, c] = sc
                maxima.append(jnp.max(sc, axis=0, keepdims=True))
            return tuple(maxima)

        def consume(j, slot, maxima, state):
            out = []
            for c, ((hh, s), bm, (m, acc)) in enumerate(zip(chains, maxima, state)):
                m_new = jnp.maximum(m, bm)
                alpha = jnp.exp2(m - m_new)
                p = jnp.exp2(s_ref[slot, c] - m_new).astype(_BF16)
                vt = jnp.concatenate([vt_ref[j, hh * head_dim:(hh + 1) * head_dim, :], ones_rows], axis=0)
                out.append((m_new, alpha * acc + _dot(vt, p)))
            return tuple(out)

        def double_trip(t, carry):
            maxima, state = carry
            j = 2 * t
            prev = jnp.where(t == 0, n_full, j - 1)
            maxima_a = scores(j, 1, False)
            state = consume(prev, 0, maxima, state)
            maxima_b = scores(j + 1, 0, False)
            state = consume(j, 1, maxima_a, state)
            return maxima_b, state

        def odd_tail(carry):
            maxima, state = carry
            prev = jnp.where(n_full == 1, n_full, n_full - 2)
            maxima_a = scores(n_full - 1, 1, False)
            state = consume(prev, 0, maxima, state)
            return consume(n_full - 1, 1, maxima_a, state)

        def even_tail(carry):
            maxima, state = carry
            last = jnp.where(n_full == 0, n_full, n_full - 1)
            return consume(last, 0, maxima, state)

        init = (jnp.full((1, ATT_TQ), MASK_VALUE, _F32), jnp.zeros((head_dim + SUM_ROWS, ATT_TQ), _F32))
        carry = (scores(n_full, 0, True), (init,) * len(chains))
        carry = lax.fori_loop(0, n_full // 2, double_trip, carry)
        final = lax.cond(n_full % 2 == 1, odd_tail, even_tail, carry)
        for (hh, s), (_, acc) in zip(chains, final):
            cols = slice((g * group + s) * ATT_TQ, (g * group + s + 1) * ATT_TQ)
            out = acc[0:head_dim, :] / acc[head_dim:head_dim + 1, :]
            o_ref[hh * head_dim:(hh + 1) * head_dim, cols] = out.astype(_BF16)


def _attention(q, k, e, vt, *, n_streams, t_q, t_k, q_off, q_row0, k_row0, head_dim):
    hd = q.shape[1]
    n_pairs = hd // LANES
    tq_blk = min(t_q, 4 * ATT_TQ)
    assert t_q % tq_blk == 0 and tq_blk % ATT_TQ == 0 and t_k % ATT_TK == 0 and q_off % ATT_TQ == 0
    assert q_row0 % tq_blk == 0 and k_row0 % t_k == 0 and q_off + t_q <= t_k
    nq = t_q // tq_blk
    qb0, kb0, vb0 = q_row0 // tq_blk, k_row0 // t_k, k_row0 // ATT_TK
    nkb = t_k // ATT_TK
    assert vb0 % nkb == 0
    n_sub = tq_blk // ATT_TQ
    n_chains = (LANES // head_dim) * _attn_group(n_sub, q_off)
    return pl.pallas_call(
        functools.partial(_attn_body, n_sub=n_sub, q_off=q_off, head_dim=head_dim),
        grid=(n_streams, n_pairs, nq),
        in_specs=[
            pl.BlockSpec((tq_blk, LANES), lambda b, p, i: (qb0 + b * nq + i, p)),
            pl.BlockSpec((t_k, LANES), lambda b, p, i: (kb0 + b, p)),
            pl.BlockSpec((None, t_k, LANES), lambda b, p, i: (b, 0, 0)),
            pl.BlockSpec((nkb, LANES, ATT_TK), lambda b, p, i: (vb0 // nkb + b, p, 0)),
        ],
        out_specs=pl.BlockSpec((LANES, tq_blk), lambda b, p, i: (p, b * nq + i)),
        out_shape=jax.ShapeDtypeStruct((hd, n_streams * t_q), _BF16),
        scratch_shapes=[pltpu.VMEM((2, n_chains, ATT_TK, ATT_TQ), _F32)],
        compiler_params=_params("arbitrary", "arbitrary", "arbitrary"),
        name="forget_attention",
    )(q, k, e, vt)


def _oproj_body(x_ref, *refs, rows):
    ot_refs, wo_ref, y_ref = refs[:len(rows)], refs[len(rows)], refs[len(rows) + 1]
    o_t = _slab_read(ot_refs, rows)
    y_ref[...] = x_ref[...] + lax.dot_general(
        o_t, wo_ref[...], (((0,), (0,)), ((), ())), preferred_element_type=_F32)


def _o_project(x, o_ts, w_o):
    n, d = x.shape
    hd = w_o.shape[0]
    rows = tuple(o.shape[1] for o in o_ts)
    assert sum(rows) == n
    row = pl.BlockSpec((TOKEN_TILE, d), lambda i: (i, 0))
    return pl.pallas_call(
        functools.partial(_oproj_body, rows=rows),
        grid=(n // TOKEN_TILE,),
        in_specs=[row] + _slab_specs(rows, hd, transposed=True) + [_const_spec(w_o.shape)],
        out_specs=row,
        out_shape=jax.ShapeDtypeStruct((n, d), _F32),
        compiler_params=_params("arbitrary"),
        name="o_project",
    )(x, *o_ts, w_o)


def kernel(x_prompt, x_sample, cache_pool, cache_k, cache_v, cache_logf, ln_ffn1, ln_mix, ln_ffn2, w_ffn_in, w_ffn_out, w_pool, pool_scale, ln_kv, w_kv, w_fgate, b_fgate, w_q, w_o, ln_final):
    batch, seq, d = x_prompt.shape
    dec_batch, dec_seq, _ = x_sample.shape
    past_len, n_heads, head_dim = cache_k.shape[1:]
    hd = n_heads * head_dim
    depth = ln_ffn1.shape[0]
    n_a = w_pool.shape[0]
    n_prompt, n_sample = batch * seq, dec_batch * dec_seq
    rows = (n_prompt, n_sample)
    assert head_dim * 2 == LANES and seq % TOKEN_TILE == 0 and TOKEN_TILE == ATT_TK

    win = w_ffn_in.astype(_BF16)
    wout = w_ffn_out.astype(_BF16)
    wpool = w_pool.astype(_BF16)
    wkv = w_kv.astype(_BF16)
    wq = w_q.astype(_BF16)
    wo = w_o.astype(_BF16)
    wfg = jnp.pad(w_fgate, ((0, 0), (0, LANES - n_heads))).astype(_BF16)
    bfg = jnp.pad(b_fgate, (0, LANES - n_heads)).reshape(1, LANES)

    sample_q = -(-dec_seq // ATT_TQ) * ATT_TQ
    sample_keys = past_len + ATT_TK
    assert sample_q <= ATT_TK

    slabs = [x_prompt.reshape(n_prompt, d), x_sample.reshape(n_sample, d)]
    new_pool_prompt, new_pool_sample = [], []
    ks = vs = logf128 = k_bf = vt = e_prompt = e_sample = k_bf_sample = vt_sample = None
    for l in range(depth):
        last = l == depth - 1
        x = _ffn(slabs, ln_ffn1[l], win[l, 0], wout[l, 0])
        if l < n_a:
            hist_p = jnp.zeros((batch, POOL_HIST_ROWS, d), _F32)
            hist_s = jnp.pad(cache_pool[l], ((0, 0), (POOL_HIST_ROWS - POOL_HIST, 0), (0, 0)))
            xp, pool_p = _pool_mix(x, hist_p, ln_mix[l], wpool[l], pool_scale[l],
                                   n_streams=batch, t_len=seq, row0=0, pos0=0)
            xs, pool_s = _pool_mix(x, hist_s, ln_mix[l], wpool[l], pool_scale[l],
                                   n_streams=dec_batch, t_len=dec_seq, row0=n_prompt, pos0=past_len)
            slabs = [xp, xs]
            new_pool_prompt.append(pool_p[:, POOL_HIST_ROWS - POOL_HIST:])
            new_pool_sample.append(pool_s[:, POOL_HIST_ROWS - POOL_HIST:])
        else:
            j = l - n_a
            q_bf = _q_project(x, ln_mix[l], wq[j], head_dim ** -0.5 * LOG2_E)
            o_prompt = _attention(q_bf, k_bf, e_prompt, vt, n_streams=batch, t_q=seq, t_k=seq, q_off=0,
                                  q_row0=0, k_row0=0, head_dim=head_dim)
            q_s = jnp.pad(q_bf[n_prompt:].reshape(dec_batch, dec_seq, hd),
                          ((0, 0), (0, sample_q - dec_seq), (0, 0))).reshape(dec_batch * sample_q, hd)
            o_sample = _attention(q_s, k_bf_sample, e_sample, vt_sample, n_streams=dec_batch, t_q=sample_q,
                                  t_k=sample_keys, q_off=past_len, q_row0=0, k_row0=0, head_dim=head_dim)
            o_sample = o_sample.reshape(hd, dec_batch, sample_q)[:, :, :dec_seq].reshape(hd, n_sample)
            slabs = [_o_project(x, [o_prompt, o_sample], wo[j])]
        x = _ffn(slabs, ln_ffn2[l], win[l, 1], wout[l, 1], ln_final if last else None, rows if last else None)
        slabs = x if last else [x]
        if l == n_a - 1:
            ks, vs, logf128, k_bf, vt = _kv_project(x, rows, ln_kv, wkv, wfg, bfg, n_heads)
            e_prompt = _decay_features(logf128[:n_prompt].reshape(batch, seq, LANES), n_heads)
            logf_s = jnp.concatenate([
                jnp.pad(cache_logf, ((0, 0), (0, 0), (0, LANES - n_heads))),
                logf128[n_prompt:].reshape(dec_batch, dec_seq, LANES),
                jnp.zeros((dec_batch, ATT_TK - dec_seq, LANES), _F32)], axis=1)
            e_sample = _decay_features(logf_s, n_heads)
            k_bf_sample, vt_sample = _cache_keys(
                cache_k, cache_v, k_bf, vs[1], new_row0=n_prompt, dec_seq=dec_seq)

    y_prompt, y_sample = slabs
    logf = logf128[:, :n_heads]
    return (y_prompt.reshape(batch, seq, d), y_sample.reshape(dec_batch, dec_seq, d),
            jnp.stack(new_pool_prompt), jnp.stack(new_pool_sample),
            ks[0].reshape(batch, seq, n_heads, head_dim), vs[0].reshape(batch, seq, n_heads, head_dim),
            logf[:n_prompt].reshape(batch, seq, n_heads),
            ks[1].reshape(dec_batch, dec_seq, n_heads, head_dim),
            vs[1].reshape(dec_batch, dec_seq, n_heads, head_dim),
            logf[n_prompt:].reshape(dec_batch, dec_seq, n_heads))
```

```python
import functools

import jax
import jax.numpy as jnp
from jax import lax
from jax.experimental import pallas as pl
from jax.experimental.pallas import tpu as pltpu

RMS_EPS = 1e-6
POOL_WINDOWS = (2, 4, 8, 16)
POOL_HIST = max(POOL_WINDOWS) - 1

LANES = 128
TOKEN_TILE = 512
FFN_CHUNK = 256
POOL_PAD = 8
POOL_HIST_ROWS = 16
CUMSUM_CHUNK = 256
ATT_TQ = 256
ATT_TK = 512
SUM_ROWS = 16
MASK_VALUE = -1e30
LOG2_E = 1.4426950408889634
VMEM_LIMIT = 52 * 1024 * 1024

_BF16 = jnp.bfloat16
_F32 = jnp.float32


def _rms(x, g):
    ms = jnp.mean(x * x, axis=-1, keepdims=True)
    return x * lax.rsqrt(ms + RMS_EPS) * g


def _dot(a, b):
    return jnp.dot(a, b, preferred_element_type=_F32)


def _split3(x):
    hi = x.astype(_BF16)
    r = x - hi.astype(_F32)
    mid = r.astype(_BF16)
    lo = (r - mid.astype(_F32)).astype(_BF16)
    return hi, mid, lo


def _params(*sem):
    return pltpu.CompilerParams(dimension_semantics=sem, vmem_limit_bytes=VMEM_LIMIT)


def _const_spec(shape):
    zeros = (0,) * len(shape)
    return pl.BlockSpec(shape, lambda *_: zeros, pipeline_mode=pl.Buffered(1))


def _slab_blocks(rows):
    assert all(n % TOKEN_TILE == 0 for n in rows)
    return [n // TOKEN_TILE for n in rows]


def _slab_specs(rows, width, transposed=False):
    tail = width if isinstance(width, tuple) else (width,)
    specs, start = [], 0
    for nb in _slab_blocks(rows):
        def index(i, start=start, nb=nb):
            blk = jnp.clip(i - start, 0, nb - 1)
            return (0, blk) if transposed else (blk,) + (0,) * len(tail)
        shape = tail + (TOKEN_TILE,) if transposed else (TOKEN_TILE,) + tail
        mode = dict(pipeline_mode=pl.Buffered(1)) if nb == 1 else {}
        specs.append(pl.BlockSpec(shape, index, **mode))
        start += nb
    return specs


def _slab_read(refs, rows):
    i = pl.program_id(0)
    blocks = _slab_blocks(rows)
    value, end = refs[-1][...], sum(blocks)
    for ref, nb in zip(refs[-2::-1], blocks[:0:-1]):
        end -= nb
        value = jnp.where(i < end, ref[...], value)
    return value


def _slab_write(refs, rows, value):
    i = pl.program_id(0)
    start = 0
    for ref, nb in zip(refs, _slab_blocks(rows)):
        @pl.when((i >= start) & (i < start + nb))
        def _(ref=ref):
            ref[...] = value
        start += nb


def _slab_shapes(rows, width, dtype):
    tail = width if isinstance(width, tuple) else (width,)
    return [jax.ShapeDtypeStruct((n,) + tail, dtype) for n in rows]


def _ffn_body(*refs, rows_in, rows_out, d_ff, final_norm):
    x_refs, refs = refs[:len(rows_in)], refs[len(rows_in):]
    g_ref, win_ref, wout_ref = refs[:3]
    refs = refs[3:]
    if final_norm:
        gf_ref, refs = refs[0], refs[1:]
    o_refs, acc_ref = refs[:len(rows_out)], refs[len(rows_out)]
    x = _slab_read(x_refs, rows_in)
    xn = _rms(x, g_ref[...]).astype(_BF16)
    for c in range(d_ff // FFN_CHUNK):
        lo, hi = c * FFN_CHUNK, (c + 1) * FFN_CHUNK
        g = _dot(xn, win_ref[:, lo:hi])
        u = _dot(xn, win_ref[:, d_ff + lo:d_ff + hi])
        a = (g * (1.0 / (1.0 + jnp.exp(-g))) * u).astype(_BF16)
        part = _dot(a, wout_ref[lo:hi, :])
        if c == 0:
            acc_ref[...] = part
        else:
            acc_ref[...] += part
    y = x + 0.5 * acc_ref[...]
    if final_norm:
        y = _rms(y, gf_ref[...])
    _slab_write(o_refs, rows_out, y)


def _ffn(xs, g, w_in, w_out, g_final=None, out_rows=None):
    d = xs[0].shape[1]
    d_ff = w_out.shape[0]
    assert d_ff % FFN_CHUNK == 0
    rows_in = tuple(x.shape[0] for x in xs)
    rows_out = tuple(out_rows) if out_rows else (sum(rows_in),)
    assert sum(rows_out) == sum(rows_in)
    in_specs = _slab_specs(rows_in, d) + [_const_spec((1, d)), _const_spec(w_in.shape), _const_spec(w_out.shape)]
    args = list(xs) + [g.reshape(1, d), w_in, w_out]
    if g_final is not None:
        in_specs.append(_const_spec((1, d)))
        args.append(g_final.reshape(1, d))
    out = pl.pallas_call(
        functools.partial(_ffn_body, rows_in=rows_in, rows_out=rows_out, d_ff=d_ff,
                          final_norm=g_final is not None),
        grid=(sum(_slab_blocks(rows_in)),),
        in_specs=in_specs,
        out_specs=_slab_specs(rows_out, d),
        out_shape=_slab_shapes(rows_out, d, _F32),
        scratch_shapes=[pltpu.VMEM((TOKEN_TILE, d), _F32)],
        compiler_params=_params("arbitrary"),
        name="ffn",
    )(*args)
    return out if out_rows else out[0]


def _pool_body(x_ref, hist_ref, g_ref, wp_ref, sc_ref, o_ref, pool_ref, buf, s2, s4, s8, *, tt, pos0):
    t = pl.program_id(1)
    d = x_ref.shape[-1]
    gw = d // len(POOL_WINDOWS)
    off = POOL_PAD + POOL_HIST_ROWS
    end = off + tt

    @pl.when(t == 0)
    def _():
        zeros = jnp.zeros((POOL_PAD, d), _F32)
        buf[0:POOL_PAD, :] = zeros
        s2[0:POOL_PAD, :] = zeros
        s4[0:POOL_PAD, :] = zeros
        buf[POOL_PAD:off, :] = hist_ref[...]

    @pl.when(t > 0)
    def _():
        buf[POOL_PAD:off, :] = buf[end - POOL_HIST_ROWS:end, :]

    x = x_ref[...]
    u = _rms(x, g_ref[...])
    buf[off:end, :] = u

    s2[POOL_PAD:end, :] = buf[POOL_PAD:end, :] + buf[POOL_PAD - 1:end - 1, :]
    s4[POOL_PAD:end, gw:] = s2[POOL_PAD:end, gw:] + s2[POOL_PAD - 2:end - 2, gw:]
    s8[POOL_PAD:end, 2 * gw:] = s4[POOL_PAD:end, 2 * gw:] + s4[POOL_PAD - 4:end - 4, 2 * gw:]
    sums = (
        s2[off:end, 0:gw],
        s4[off:end, gw:2 * gw],
        s8[off:end, 2 * gw:3 * gw],
        s8[off:end, 3 * gw:] + s8[off - 8:end - 8, 3 * gw:],
    )
    pos = pos0 + t * tt + lax.broadcasted_iota(jnp.int32, (tt, 1), 0)
    for gi, w in enumerate(POOL_WINDOWS):
        sl = slice(gi * gw, (gi + 1) * gw)
        inv_cnt = 1.0 / jnp.minimum(pos + 1, w).astype(_F32)
        diff = (sums[gi] * inv_cnt - u[:, sl]).astype(_BF16)
        y = _dot(diff, wp_ref[gi])
        o_ref[:, sl] = x[:, sl] + y * sc_ref[:, sl]

    @pl.when(t == pl.num_programs(1) - 1)
    def _():
        pool_ref[...] = buf[end - POOL_HIST_ROWS:end, :]


def _pool_mix(x_all, hist16, g, w_pool, scale, *, n_streams, t_len, row0, pos0):
    d = x_all.shape[1]
    tt = min(TOKEN_TILE, t_len)
    assert t_len % tt == 0 and row0 % tt == 0
    nt = t_len // tt
    blk0 = row0 // tt
    hist = pl.BlockSpec((None, POOL_HIST_ROWS, d), lambda b, t: (b, 0, 0))
    rows_buf = POOL_PAD + POOL_HIST_ROWS + tt
    return pl.pallas_call(
        functools.partial(_pool_body, tt=tt, pos0=pos0),
        grid=(n_streams, nt),
        in_specs=[pl.BlockSpec((tt, d), lambda b, t: (blk0 + b * nt + t, 0)), hist, _const_spec((1, d)),
                  _const_spec(w_pool.shape), _const_spec((1, d))],
        out_specs=[pl.BlockSpec((tt, d), lambda b, t: (b * nt + t, 0)), hist],
        out_shape=[jax.ShapeDtypeStruct((n_streams * t_len, d), _F32),
                   jax.ShapeDtypeStruct((n_streams, POOL_HIST_ROWS, d), _F32)],
        scratch_shapes=[pltpu.VMEM((rows_buf, d), _F32)] * 4,
        compiler_params=_params("arbitrary", "arbitrary"),
        name="pool_mix",
    )(x_all, hist16, g.reshape(1, d), w_pool, scale.reshape(1, d))


def _kv_body(x_ref, g_ref, wkv_ref, wf_ref, bf_ref, *out_refs, rows, hd, n_heads):
    ns = len(rows)
    kt_refs, vt_refs = out_refs[:ns], out_refs[ns:2 * ns]
    lf_ref, kb_ref, vtile_ref = out_refs[2 * ns:]
    xn = _rms(x_ref[...], g_ref[...]).astype(_BF16)
    kv = _dot(xn, wkv_ref[...])
    kv_t = kv.T
    _slab_write(kt_refs, rows, kv_t[0:hd, :])
    _slab_write(vt_refs, rows, kv_t[hd:2 * hd, :])
    kb_ref[...] = kv[:, 0:hd].astype(_BF16)
    vtile_ref[...] = kv_t[hd:2 * hd, :].astype(_BF16)
    z = _dot(xn, wf_ref[...]) + bf_ref[...]
    logf = jnp.minimum(z, 0.0) - jnp.log1p(jnp.exp(-jnp.abs(z)))
    lane = lax.broadcasted_iota(jnp.int32, logf.shape, 1)
    lf_ref[...] = jnp.where(lane < n_heads, logf, 0.0)


def _kv_project(x, streams, g, w_kv, w_fg_pad, b_fg_pad, n_heads):
    n, d = x.shape
    hd = w_kv.shape[1] // 2
    nb = n // TOKEN_TILE
    rows = tuple(s * t for s, t in streams)
    shapes = [(s, t) if t % TOKEN_TILE == 0 else (1, s * t) for s, t in streams]
    row = lambda width: pl.BlockSpec((TOKEN_TILE, width), lambda i: (i, 0))

    def transposed_specs():
        specs, start = [], 0
        for (_, cols), nblk in zip(shapes, _slab_blocks(rows)):
            def index(i, start=start, nblk=nblk, nt=cols // TOKEN_TILE):
                blk = jnp.clip(i - start, 0, nblk - 1)
                return (blk // nt, 0, blk % nt)
            mode = dict(pipeline_mode=pl.Buffered(1)) if nblk == 1 else {}
            specs.append(pl.BlockSpec((None, hd, TOKEN_TILE), index, **mode))
            start += nblk
        return specs

    transposed_shapes = [jax.ShapeDtypeStruct((s, hd, cols), _F32) for s, cols in shapes]
    out = pl.pallas_call(
        functools.partial(_kv_body, rows=rows, hd=hd, n_heads=n_heads),
        grid=(nb,),
        in_specs=[row(d), _const_spec((1, d)), _const_spec(w_kv.shape), _const_spec(w_fg_pad.shape),
                  _const_spec((1, LANES))],
        out_specs=transposed_specs() + transposed_specs() + [
            row(LANES), row(hd), pl.BlockSpec((None, hd, TOKEN_TILE), lambda i: (i, 0, 0))],
        out_shape=transposed_shapes + transposed_shapes + [
            jax.ShapeDtypeStruct((n, LANES), _F32), jax.ShapeDtypeStruct((n, hd), _BF16),
            jax.ShapeDtypeStruct((nb, hd, TOKEN_TILE), _BF16)],
        compiler_params=_params("arbitrary"),
        name="kv_project",
    )(x, g.reshape(1, d), w_kv, w_fg_pad, b_fg_pad)
    ns = len(rows)
    return out[:ns], out[ns:2 * ns], out[2 * ns], out[2 * ns + 1], out[2 * ns + 2]


def _cache_body(ckt_ref, cvt_ref, kn_ref, vn_ref, k_ref, vt_ref, *, n_cache_blocks, dec_seq, new_row0):
    b = pl.program_id(0)
    j = pl.program_id(1)

    @pl.when(j < n_cache_blocks)
    def _():
        k_ref[...] = ckt_ref[...].T.astype(_BF16)
        vt_ref[...] = cvt_ref[...].astype(_BF16)

    @pl.when(j >= n_cache_blocks)
    def _():
        hd = k_ref.shape[1]
        pad = jnp.zeros((ATT_TK - dec_seq, hd), _F32)
        k_ref[...] = jnp.concatenate([kn_ref[...].astype(_F32), pad], axis=0).astype(_BF16)
        off = (new_row0 + b * dec_seq) % TOKEN_TILE
        r = lax.broadcasted_iota(jnp.int32, (TOKEN_TILE, ATT_TK), 0)
        c = lax.broadcasted_iota(jnp.int32, (TOKEN_TILE, ATT_TK), 1)
        select = jnp.where((r == off + c) & (c < dec_seq), 1.0, 0.0).astype(_BF16)
        vt_ref[...] = _dot(vn_ref[...], select).astype(_BF16)


def _cache_keys(cache_kt, cache_vt, k_new_bf, vt_new, *, new_row0, dec_seq):
    s, hd, past = cache_kt.shape
    assert past % ATT_TK == 0 and dec_seq % 16 == 0 and TOKEN_TILE % dec_seq == 0 and new_row0 % dec_seq == 0
    ncb = past // ATT_TK
    nkb = ncb + 1
    cache = pl.BlockSpec((None, hd, ATT_TK), lambda b, j: (b, 0, jnp.minimum(j, ncb - 1)))
    return pl.pallas_call(
        functools.partial(_cache_body, n_cache_blocks=ncb, dec_seq=dec_seq, new_row0=new_row0),
        grid=(s, nkb),
        in_specs=[cache, cache,
                  pl.BlockSpec((dec_seq, hd), lambda b, j: (new_row0 // dec_seq + b, 0)),
                  pl.BlockSpec((None, hd, TOKEN_TILE), lambda b, j: ((new_row0 + b * dec_seq) // TOKEN_TILE, 0, 0))],
        out_specs=[pl.BlockSpec((ATT_TK, hd), lambda b, j: (b * nkb + j, 0)),
                   pl.BlockSpec((None, hd, ATT_TK), lambda b, j: (b * nkb + j, 0, 0))],
        out_shape=[jax.ShapeDtypeStruct((s * nkb * ATT_TK, hd), _BF16),
                   jax.ShapeDtypeStruct((s * nkb, hd, ATT_TK), _BF16)],
        compiler_params=_params("arbitrary", "arbitrary"),
        name="cache_keys",
    )(cache_kt, cache_vt, k_new_bf, vt_new)


def _decay_body(lf_ref, e_ref, c_ref, *, n_heads):
    n_chunks = lf_ref.shape[0] // CUMSUM_CHUNK
    r = lax.broadcasted_iota(jnp.int32, (CUMSUM_CHUNK, CUMSUM_CHUNK), 0)
    c = lax.broadcasted_iota(jnp.int32, (CUMSUM_CHUNK, CUMSUM_CHUNK), 1)
    tril = jnp.where(r >= c, 1.0, 0.0).astype(_BF16)
    sr = lax.broadcasted_iota(jnp.int32, (LANES, LANES), 0)
    sc = lax.broadcasted_iota(jnp.int32, (LANES, LANES), 1)
    place = [jnp.where((sc == 3 * sr + j) & (sr < n_heads), 1.0, 0.0).astype(_BF16) for j in range(3)]

    for i in range(n_chunks):
        rows = slice(i * CUMSUM_CHUNK, (i + 1) * CUMSUM_CHUNK)
        hi, mid, lo = _split3(lf_ref[rows, :])
        c_ref[rows, :] = _dot(tril, hi) + _dot(tril, mid) + _dot(tril, lo)
    carry = jnp.zeros((1, LANES), _F32)
    for i in range(n_chunks):
        rows = slice(i * CUMSUM_CHUNK, (i + 1) * CUMSUM_CHUNK)
        csum = c_ref[rows, :] + carry
        carry = carry + c_ref[(i + 1) * CUMSUM_CHUNK - 1:(i + 1) * CUMSUM_CHUNK, :]
        p0, p1, p2 = _split3(-LOG2_E * csum)
        feats = _dot(p0, place[0]) + _dot(p1, place[1]) + _dot(p2, place[2])
        e_ref[rows, :] = feats.astype(_BF16)


def _decay_features(logf, n_heads):
    b, t, _ = logf.shape
    assert t % CUMSUM_CHUNK == 0 and 3 * n_heads <= LANES
    blk = pl.BlockSpec((None, t, LANES), lambda i: (i, 0, 0))
    return pl.pallas_call(
        functools.partial(_decay_body, n_heads=n_heads),
        grid=(b,),
        in_specs=[blk],
        out_specs=blk,
        out_shape=jax.ShapeDtypeStruct((b, t, LANES), _BF16),
        scratch_shapes=[pltpu.VMEM((t, LANES), _F32)],
        compiler_params=_params("arbitrary"),
        name="decay_features",
    )(logf)


def _q_body(x_ref, g_ref, wq_ref, q_ref, *, scale):
    u = _rms(x_ref[...], g_ref[...]).astype(_BF16)
    q_ref[...] = (_dot(u, wq_ref[...]) * scale).astype(_BF16)


def _q_project(x, g, w_q, scale):
    n, d = x.shape
    hd = w_q.shape[1]
    return pl.pallas_call(
        functools.partial(_q_body, scale=scale),
        grid=(n // TOKEN_TILE,),
        in_specs=[pl.BlockSpec((TOKEN_TILE, d), lambda i: (i, 0)), _const_spec((1, d)), _const_spec(w_q.shape)],
        out_specs=pl.BlockSpec((TOKEN_TILE, hd), lambda i: (i, 0)),
        out_shape=jax.ShapeDtypeStruct((n, hd), _BF16),
        compiler_params=_params("arbitrary"),
        name="q_project",
    )(x, g.reshape(1, d), w_q)


def _attn_group(n_sub, q_off):
    return ATT_TK // ATT_TQ if (n_sub * ATT_TQ) % ATT_TK == 0 and q_off % ATT_TK == 0 else 1


def _attn_body(q_ref, k_ref, e_ref, vt_ref, o_ref, s_ref, *, n_sub, q_off, head_dim):
    pair = pl.program_id(1)
    qblk = pl.program_id(2)
    heads_per_block = LANES // head_dim
    group = _attn_group(n_sub, q_off)
    lane = lax.broadcasted_iota(jnp.int32, (ATT_TQ, LANES), 1)
    feat = lax.broadcasted_iota(jnp.int32, (LANES, ATT_TQ), 0)
    key_iota = lax.broadcasted_iota(jnp.int32, (ATT_TK, ATT_TQ), 0)
    qry_iota = lax.broadcasted_iota(jnp.int32, (ATT_TK, ATT_TQ), 1)
    chains = [(hh, s) for hh in range(heads_per_block) for s in range(group)]
    ones_rows = jnp.ones((SUM_ROWS, ATT_TK), _BF16)

    for g in range(n_sub // group):
        q0 = q_off + (qblk * n_sub + g * group) * ATT_TQ
        n_full = q0 // ATT_TK
        q_augs = []
        for hh, s in chains:
            head = pair * heads_per_block + hh
            bias_sel = jnp.where((feat >= 3 * head) & (feat < 3 * head + 3), 1.0, 0.0).astype(_BF16)
            head_lanes = (lane >= hh * head_dim) & (lane < (hh + 1) * head_dim)
            q = q_ref[(g * group + s) * ATT_TQ:(g * group + s + 1) * ATT_TQ, :].astype(_F32)
            q_t = jnp.where(head_lanes, q, 0.0).T.astype(_BF16)
            q_augs.append(jnp.concatenate([q_t, bias_sel], axis=0))

        def scores(j, slot, masked):
            rows = pl.ds(pl.multiple_of(j * ATT_TK, ATT_TK), ATT_TK)
            k_aug = jnp.concatenate([k_ref[rows, :], e_ref[rows, :]], axis=1)
            tiles = [_dot(k_aug, q_aug) for q_aug in q_augs]
            maxima = []
            for c, ((hh, s), sc) in enumerate(zip(chains, tiles)):
                if masked:
                    sc = jnp.where(key_iota + j * ATT_TK <= qry_iota + (q0 + s * ATT_TQ), sc, MASK_VALUE)
                s_ref[slot, c] = sc
                maxima.append(jnp.max(sc, axis=0, keepdims=True))
            return tuple(maxima)

        def consume(j, slot, maxima, state):
            out = []
            for c, ((hh, s), bm, (m, acc)) in enumerate(zip(chains, maxima, state)):
                m_new = jnp.maximum(m, bm)
                alpha = jnp.exp2(m - m_new)
                p = jnp.exp2(s_ref[slot, c] - m_new).astype(_BF16)
                vt = jnp.concatenate([vt_ref[j, hh * head_dim:(hh + 1) * head_dim, :], ones_rows], axis=0)
                out.append((m_new, alpha * acc + _dot(vt, p)))
            return tuple(out)

        def double_trip(t, carry):
            maxima, state = carry
            j = 2 * t
            prev = jnp.where(t == 0, n_full, j - 1)
            maxima_a = scores(j, 1, False)
            state = consume(prev, 0, maxima, state)
            maxima_b = scores(j + 1, 0, False)
            state = consume(j, 1, maxima_a, state)
            return maxima_b, state

        def odd_tail(carry):
            maxima, state = carry
            prev = jnp.where(n_full == 1, n_full, n_full - 2)
            maxima_a = scores(n_full - 1, 1, False)
            state = consume(prev, 0, maxima, state)
            return consume(n_full - 1, 1, maxima_a, state)

        def even_tail(carry):
            maxima, state = carry
            last = jnp.where(n_full == 0, n_full, n_full - 1)
            return consume(last, 0, maxima, state)

        init = (jnp.full((1, ATT_TQ), MASK_VALUE, _F32), jnp.zeros((head_dim + SUM_ROWS, ATT_TQ), _F32))
        carry = (scores(n_full, 0, True), (init,) * len(chains))
        carry = lax.fori_loop(0, n_full // 2, double_trip, carry)
        final = lax.cond(n_full % 2 == 1, odd_tail, even_tail, carry)
        for (hh, s), (_, acc) in zip(chains, final):
            cols = slice((g * group + s) * ATT_TQ, (g * group + s + 1) * ATT_TQ)
            out = acc[0:head_dim, :] / acc[head_dim:head_dim + 1, :]
            o_ref[hh * head_dim:(hh + 1) * head_dim, cols] = out.astype(_BF16)


def _attention(q, k, e, vt, *, n_streams, t_q, t_k, q_off, q_row0, k_row0, head_dim):
    hd = q.shape[1]
    n_pairs = hd // LANES
    tq_blk = min(t_q, 4 * ATT_TQ)
    assert t_q % tq_blk == 0 and tq_blk % ATT_TQ == 0 and t_k % ATT_TK == 0 and q_off % ATT_TQ == 0
    assert q_row0 % tq_blk == 0 and k_row0 % t_k == 0 and q_off + t_q <= t_k
    nq = t_q // tq_blk
    qb0, kb0, vb0 = q_row0 // tq_blk, k_row0 // t_k, k_row0 // ATT_TK
    nkb = t_k // ATT_TK
    assert vb0 % nkb == 0
    n_sub = tq_blk // ATT_TQ
    n_chains = (LANES // head_dim) * _attn_group(n_sub, q_off)
    return pl.pallas_call(
        functools.partial(_attn_body, n_sub=n_sub, q_off=q_off, head_dim=head_dim),
        grid=(n_streams, n_pairs, nq),
        in_specs=[
            pl.BlockSpec((tq_blk, LANES), lambda b, p, i: (qb0 + b * nq + i, p)),
            pl.BlockSpec((t_k, LANES), lambda b, p, i: (kb0 + b, p)),
            pl.BlockSpec((None, t_k, LANES), lambda b, p, i: (b, 0, 0)),
            pl.BlockSpec((nkb, LANES, ATT_TK), lambda b, p, i: (vb0 // nkb + b, p, 0)),
        ],
        out_specs=pl.BlockSpec((LANES, tq_blk), lambda b, p, i: (p, b * nq + i)),
        out_shape=jax.ShapeDtypeStruct((hd, n_streams * t_q), _BF16),
        scratch_shapes=[pltpu.VMEM((2, n_chains, ATT_TK, ATT_TQ), _F32)],
        compiler_params=_params("arbitrary", "arbitrary", "arbitrary"),
        name="forget_attention",
    )(q, k, e, vt)


def _oproj_body(x_ref, *refs, rows):
    ot_refs, wo_ref, y_ref = refs[:len(rows)], refs[len(rows)], refs[len(rows) + 1]
    o_t = _slab_read(ot_refs, rows)
    y_ref[...] = x_ref[...] + lax.dot_general(
        o_t, wo_ref[...], (((0,), (0,)), ((), ())), preferred_element_type=_F32)


def _o_project(x, o_ts, w_o):
    n, d = x.shape
    hd = w_o.shape[0]
    rows = tuple(o.shape[1] for o in o_ts)
    assert sum(rows) == n
    row = pl.BlockSpec((TOKEN_TILE, d), lambda i: (i, 0))
    return pl.pallas_call(
        functools.partial(_oproj_body, rows=rows),
        grid=(n // TOKEN_TILE,),
        in_specs=[row] + _slab_specs(rows, hd, transposed=True) + [_const_spec(w_o.shape)],
        out_specs=row,
        out_shape=jax.ShapeDtypeStruct((n, d), _F32),
        compiler_params=_params("arbitrary"),
        name="o_project",
    )(x, *o_ts, w_o)


def kernel(x_prompt, x_sample, cache_pool, cache_k, cache_v, cache_logf, ln_ffn1, ln_mix, ln_ffn2, w_ffn_in, w_ffn_out, w_pool, pool_scale, ln_kv, w_kv, w_fgate, b_fgate, w_q, w_o, ln_final):
    batch, seq, d = x_prompt.shape
    dec_batch, dec_seq, _ = x_sample.shape
    past_len, n_heads, head_dim = cache_k.shape[1:]
    hd = n_heads * head_dim
    depth = ln_ffn1.shape[0]
    n_a = w_pool.shape[0]
    n_prompt, n_sample = batch * seq, dec_batch * dec_seq
    rows = (n_prompt, n_sample)
    assert head_dim * 2 == LANES and seq % TOKEN_TILE == 0 and TOKEN_TILE == ATT_TK

    win = w_ffn_in.astype(_BF16)
    wout = w_ffn_out.astype(_BF16)
    wpool = w_pool.astype(_BF16)
    wkv = w_kv.astype(_BF16)
    wq = w_q.astype(_BF16)
    wo = w_o.astype(_BF16)
    wfg = jnp.pad(w_fgate, ((0, 0), (0, LANES - n_heads))).astype(_BF16)
    bfg = jnp.pad(b_fgate, (0, LANES - n_heads)).reshape(1, LANES)

    def time_minor(cache):
        return cache.transpose(0, 2, 3, 1).reshape(cache.shape[0], hd, cache.shape[1])

    def time_major(x_t, n_streams, t_len):
        g = x_t.shape[0]
        x_t = x_t.reshape(g, n_heads, head_dim, n_streams // g, t_len)
        return x_t.transpose(0, 3, 4, 1, 2).reshape(n_streams, t_len, n_heads, head_dim)

    sample_q = -(-dec_seq // ATT_TQ) * ATT_TQ
    sample_keys = past_len + ATT_TK
    assert sample_q <= ATT_TK

    slabs = [x_prompt.reshape(n_prompt, d), x_sample.reshape(n_sample, d)]
    new_pool_prompt, new_pool_sample = [], []
    ks = vs = logf128 = k_bf = vt = e_prompt = e_sample = k_bf_sample = vt_sample = None
    for l in range(depth):
        last = l == depth - 1
        x = _ffn(slabs, ln_ffn1[l], win[l, 0], wout[l, 0])
        if l < n_a:
            hist_p = jnp.zeros((batch, POOL_HIST_ROWS, d), _F32)
            hist_s = jnp.pad(cache_pool[l], ((0, 0), (POOL_HIST_ROWS - POOL_HIST, 0), (0, 0)))
            xp, pool_p = _pool_mix(x, hist_p, ln_mix[l], wpool[l], pool_scale[l],
                                   n_streams=batch, t_len=seq, row0=0, pos0=0)
            xs, pool_s = _pool_mix(x, hist_s, ln_mix[l], wpool[l], pool_scale[l],
                                   n_streams=dec_batch, t_len=dec_seq, row0=n_prompt, pos0=past_len)
            slabs = [xp, xs]
            new_pool_prompt.append(pool_p[:, POOL_HIST_ROWS - POOL_HIST:])
            new_pool_sample.append(pool_s[:, POOL_HIST_ROWS - POOL_HIST:])
        else:
            j = l - n_a
            q_bf = _q_project(x, ln_mix[l], wq[j], head_dim ** -0.5 * LOG2_E)
            o_prompt = _attention(q_bf, k_bf, e_prompt, vt, n_streams=batch, t_q=seq, t_k=seq, q_off=0,
                                  q_row0=0, k_row0=0, head_dim=head_dim)
            q_s = jnp.pad(q_bf[n_prompt:].reshape(dec_batch, dec_seq, hd),
                          ((0, 0), (0, sample_q - dec_seq), (0, 0))).reshape(dec_batch * sample_q, hd)
            o_sample = _attention(q_s, k_bf_sample, e_sample, vt_sample, n_streams=dec_batch, t_q=sample_q,
                                  t_k=sample_keys, q_off=past_len, q_row0=0, k_row0=0, head_dim=head_dim)
            o_sample = o_sample.reshape(hd, dec_batch, sample_q)[:, :, :dec_seq].reshape(hd, n_sample)
            slabs = [_o_project(x, [o_prompt, o_sample], wo[j])]
        x = _ffn(slabs, ln_ffn2[l], win[l, 1], wout[l, 1], ln_final if last else None, rows if last else None)
        slabs = x if last else [x]
        if l == n_a - 1:
            ks, vs, logf128, k_bf, vt = _kv_project(x, [(batch, seq), (dec_batch, dec_seq)], ln_kv, wkv,
                                                    wfg, bfg, n_heads)
            e_prompt = _decay_features(logf128[:n_prompt].reshape(batch, seq, LANES), n_heads)
            logf_s = jnp.concatenate([
                jnp.pad(cache_logf, ((0, 0), (0, 0), (0, LANES - n_heads))),
                logf128[n_prompt:].reshape(dec_batch, dec_seq, LANES),
                jnp.zeros((dec_batch, ATT_TK - dec_seq, LANES), _F32)], axis=1)
            e_sample = _decay_features(logf_s, n_heads)
            k_bf_sample, vt_sample = _cache_keys(
                time_minor(cache_k), time_minor(cache_v), k_bf, vt, new_row0=n_prompt, dec_seq=dec_seq)

    y_prompt, y_sample = slabs
    logf = logf128[:, :n_heads]
    return (y_prompt.reshape(batch, seq, d), y_sample.reshape(dec_batch, dec_seq, d),
            jnp.stack(new_pool_prompt), jnp.stack(new_pool_sample),
            time_major(ks[0], batch, seq), time_major(vs[0], batch, seq),
            logf[:n_prompt].reshape(batch, seq, n_heads),
            time_major(ks[1], dec_batch, dec_seq), time_major(vs[1], dec_batch, dec_seq),
            logf[n_prompt:].reshape(dec_batch, dec_seq, n_heads))
```

```python
import functools

import jax
import jax.numpy as jnp
from jax import lax
from jax.experimental import pallas as pl
from jax.experimental.pallas import tpu as pltpu

RMS_EPS = 1e-6
POOL_WINDOWS = (2, 4, 8, 16)
POOL_HIST = max(POOL_WINDOWS) - 1

LANES = 128
TOKEN_TILE = 512
FFN_CHUNK = 256
POOL_PAD = 8
POOL_HIST_ROWS = 16
CUMSUM_CHUNK = 256
ATT_TQ = 256
ATT_TK = 512
SUM_ROWS = 16
MASK_VALUE = -1e30
LOG2_E = 1.4426950408889634
VMEM_LIMIT = 52 * 1024 * 1024

_BF16 = jnp.bfloat16
_F32 = jnp.float32


def _rms(x, g):
    ms = jnp.mean(x * x, axis=-1, keepdims=True)
    return x * lax.rsqrt(ms + RMS_EPS) * g


def _dot(a, b):
    return jnp.dot(a, b, preferred_element_type=_F32)


def _split3(x):
    hi = x.astype(_BF16)
    r = x - hi.astype(_F32)
    mid = r.astype(_BF16)
    lo = (r - mid.astype(_F32)).astype(_BF16)
    return hi, mid, lo


def _params(*sem):
    return pltpu.CompilerParams(dimension_semantics=sem, vmem_limit_bytes=VMEM_LIMIT)


def _const_spec(shape):
    zeros = (0,) * len(shape)
    return pl.BlockSpec(shape, lambda *_: zeros, pipeline_mode=pl.Buffered(1))


def _slab_blocks(rows):
    assert all(n % TOKEN_TILE == 0 for n in rows)
    return [n // TOKEN_TILE for n in rows]


def _slab_specs(rows, width, transposed=False):
    tail = width if isinstance(width, tuple) else (width,)
    specs, start = [], 0
    for nb in _slab_blocks(rows):
        def index(i, start=start, nb=nb):
            blk = jnp.clip(i - start, 0, nb - 1)
            return (0, blk) if transposed else (blk,) + (0,) * len(tail)
        shape = tail + (TOKEN_TILE,) if transposed else (TOKEN_TILE,) + tail
        mode = dict(pipeline_mode=pl.Buffered(1)) if nb == 1 else {}
        specs.append(pl.BlockSpec(shape, index, **mode))
        start += nb
    return specs


def _slab_read(refs, rows):
    i = pl.program_id(0)
    blocks = _slab_blocks(rows)
    value, end = refs[-1][...], sum(blocks)
    for ref, nb in zip(refs[-2::-1], blocks[:0:-1]):
        end -= nb
        value = jnp.where(i < end, ref[...], value)
    return value


def _slab_write(refs, rows, value):
    i = pl.program_id(0)
    start = 0
    for ref, nb in zip(refs, _slab_blocks(rows)):
        @pl.when((i >= start) & (i < start + nb))
        def _(ref=ref):
            ref[...] = value
        start += nb


def _slab_shapes(rows, width, dtype):
    tail = width if isinstance(width, tuple) else (width,)
    return [jax.ShapeDtypeStruct((n,) + tail, dtype) for n in rows]


def _ffn_body(*refs, rows_in, rows_attn, rows_out, d_ff, final_norm, query_scale):
    refs = list(refs)
    take = lambda n: [refs.pop(0) for _ in range(n)]
    x_refs = take(len(rows_in))
    if rows_attn:
        ot_refs, (wo_ref,) = take(len(rows_attn)), take(1)
    g_ref, win_ref, wout_ref = take(3)
    if final_norm:
        (gf_ref,) = take(1)
    if query_scale is not None:
        gq_ref, wq_ref = take(2)
    o_refs = take(len(rows_out))
    if query_scale is not None:
        (q_ref,) = take(1)
    (acc_ref,) = refs

    x = _slab_read(x_refs, rows_in)
    if rows_attn:
        o_t = _slab_read(ot_refs, rows_attn)
        x = x + lax.dot_general(o_t, wo_ref[...], (((0,), (0,)), ((), ())), preferred_element_type=_F32)
    xn = _rms(x, g_ref[...]).astype(_BF16)
    for c in range(d_ff // FFN_CHUNK):
        lo, hi = c * FFN_CHUNK, (c + 1) * FFN_CHUNK
        g = _dot(xn, win_ref[:, lo:hi])
        u = _dot(xn, win_ref[:, d_ff + lo:d_ff + hi])
        a = (g * (1.0 / (1.0 + jnp.exp(-g))) * u).astype(_BF16)
        part = _dot(a, wout_ref[lo:hi, :])
        if c == 0:
            acc_ref[...] = part
        else:
            acc_ref[...] += part
    y = x + 0.5 * acc_ref[...]
    if query_scale is not None:
        u_q = _rms(y, gq_ref[...]).astype(_BF16)
        q_ref[...] = (_dot(u_q, wq_ref[...]) * query_scale).astype(_BF16)
    if final_norm:
        y = _rms(y, gf_ref[...])
    _slab_write(o_refs, rows_out, y)


def _ffn(xs, g, w_in_all, w_out_all, which, *, attn=None, query=None, g_final=None, out_rows=None):
    d = xs[0].shape[1]
    d_ff = w_out_all.shape[2]
    assert d_ff % FFN_CHUNK == 0
    rows_in = tuple(x.shape[0] for x in xs)
    rows_out = tuple(out_rows) if out_rows else (sum(rows_in),)
    assert sum(rows_out) == sum(rows_in)
    n = sum(rows_in)
    picked = lambda w: pl.BlockSpec((None, None) + w.shape[2:], lambda i: tuple(which) + (0, 0),
                                    pipeline_mode=pl.Buffered(1))
    in_specs, args, rows_attn = _slab_specs(rows_in, d), list(xs), ()
    if attn is not None:
        o_ts, w_o = attn
        rows_attn = tuple(o.shape[1] for o in o_ts)
        assert sum(rows_attn) == n
        in_specs += _slab_specs(rows_attn, w_o.shape[0], transposed=True) + [_const_spec(w_o.shape)]
        args += list(o_ts) + [w_o]
    in_specs += [_const_spec((1, d)), picked(w_in_all), picked(w_out_all)]
    args += [g.reshape(1, d), w_in_all, w_out_all]
    if g_final is not None:
        in_specs.append(_const_spec((1, d)))
        args.append(g_final.reshape(1, d))
    out_specs, out_shape = _slab_specs(rows_out, d), _slab_shapes(rows_out, d, _F32)
    if query is not None:
        g_q, w_q, scale = query
        in_specs += [_const_spec((1, d)), _const_spec(w_q.shape)]
        args += [g_q.reshape(1, d), w_q]
        out_specs = out_specs + [pl.BlockSpec((TOKEN_TILE, w_q.shape[1]), lambda i: (i, 0))]
        out_shape = out_shape + [jax.ShapeDtypeStruct((n, w_q.shape[1]), _BF16)]
    out = pl.pallas_call(
        functools.partial(_ffn_body, rows_in=rows_in, rows_attn=rows_attn, rows_out=rows_out, d_ff=d_ff,
                          final_norm=g_final is not None, query_scale=None if query is None else query[2]),
        grid=(n // TOKEN_TILE,),
        in_specs=in_specs,
        out_specs=out_specs,
        out_shape=out_shape,
        scratch_shapes=[pltpu.VMEM((TOKEN_TILE, d), _F32)],
        compiler_params=_params("arbitrary"),
        name="ffn",
    )(*args)
    y = out[:len(rows_out)] if out_rows else out[0]
    return (y, out[-1]) if query is not None else y


def _pool_body(x_ref, hist_ref, g_ref, wp_ref, sc_ref, o_ref, pool_ref, buf, s2, s4, s8, *, tt, pos0):
    t = pl.program_id(1)
    d = x_ref.shape[-1]
    gw = d // len(POOL_WINDOWS)
    off = POOL_PAD + POOL_HIST_ROWS
    end = off + tt

    @pl.when(t == 0)
    def _():
        zeros = jnp.zeros((POOL_PAD, d), _F32)
        buf[0:POOL_PAD, :] = zeros
        s2[0:POOL_PAD, :] = zeros
        s4[0:POOL_PAD, :] = zeros
        buf[POOL_PAD:off, :] = hist_ref[...]

    @pl.when(t > 0)
    def _():
        buf[POOL_PAD:off, :] = buf[end - POOL_HIST_ROWS:end, :]

    x = x_ref[...]
    u = _rms(x, g_ref[...])
    buf[off:end, :] = u

    s2[POOL_PAD:end, :] = buf[POOL_PAD:end, :] + buf[POOL_PAD - 1:end - 1, :]
    s4[POOL_PAD:end, gw:] = s2[POOL_PAD:end, gw:] + s2[POOL_PAD - 2:end - 2, gw:]
    s8[POOL_PAD:end, 2 * gw:] = s4[POOL_PAD:end, 2 * gw:] + s4[POOL_PAD - 4:end - 4, 2 * gw:]
    sums = (
        s2[off:end, 0:gw],
        s4[off:end, gw:2 * gw],
        s8[off:end, 2 * gw:3 * gw],
        s8[off:end, 3 * gw:] + s8[off - 8:end - 8, 3 * gw:],
    )
    pos = pos0 + t * tt + lax.broadcasted_iota(jnp.int32, (tt, 1), 0)
    for gi, w in enumerate(POOL_WINDOWS):
        sl = slice(gi * gw, (gi + 1) * gw)
        inv_cnt = 1.0 / jnp.minimum(pos + 1, w).astype(_F32)
        diff = (sums[gi] * inv_cnt - u[:, sl]).astype(_BF16)
        y = _dot(diff, wp_ref[gi])
        o_ref[:, sl] = x[:, sl] + y * sc_ref[:, sl]

    @pl.when(t == pl.num_programs(1) - 1)
    def _():
        pool_ref[...] = buf[end - POOL_HIST_ROWS:end, :]


def _pool_mix(x_all, hist16, g, w_pool, scale, *, n_streams, t_len, row0, pos0):
    d = x_all.shape[1]
    tt = min(TOKEN_TILE, t_len)
    assert t_len % tt == 0 and row0 % tt == 0
    nt = t_len // tt
    blk0 = row0 // tt
    hist = pl.BlockSpec((None, POOL_HIST_ROWS, d), lambda b, t: (b, 0, 0))
    rows_buf = POOL_PAD + POOL_HIST_ROWS + tt
    return pl.pallas_call(
        functools.partial(_pool_body, tt=tt, pos0=pos0),
        grid=(n_streams, nt),
        in_specs=[pl.BlockSpec((tt, d), lambda b, t: (blk0 + b * nt + t, 0)), hist, _const_spec((1, d)),
                  _const_spec(w_pool.shape), _const_spec((1, d))],
        out_specs=[pl.BlockSpec((tt, d), lambda b, t: (b * nt + t, 0)), hist],
        out_shape=[jax.ShapeDtypeStruct((n_streams * t_len, d), _F32),
                   jax.ShapeDtypeStruct((n_streams, POOL_HIST_ROWS, d), _F32)],
        scratch_shapes=[pltpu.VMEM((rows_buf, d), _F32)] * 4,
        compiler_params=_params("arbitrary", "arbitrary"),
        name="pool_mix",
    )(x_all, hist16, g.reshape(1, d), w_pool, scale.reshape(1, d))


def _kv_body(x_ref, g_ref, wkv_ref, wf_ref, bf_ref, *out_refs, rows, hd, n_heads):
    ns = len(rows)
    kt_refs, vt_refs = out_refs[:ns], out_refs[ns:2 * ns]
    lf_ref, kb_ref, vtile_ref = out_refs[2 * ns:]
    xn = _rms(x_ref[...], g_ref[...]).astype(_BF16)
    kv = _dot(xn, wkv_ref[...])
    kv_t = kv.T
    _slab_write(kt_refs, rows, kv_t[0:hd, :])
    _slab_write(vt_refs, rows, kv_t[hd:2 * hd, :])
    kb_ref[...] = kv[:, 0:hd].astype(_BF16)
    vtile_ref[...] = kv_t[hd:2 * hd, :].astype(_BF16)
    z = _dot(xn, wf_ref[...]) + bf_ref[...]
    logf = jnp.minimum(z, 0.0) - jnp.log1p(jnp.exp(-jnp.abs(z)))
    lane = lax.broadcasted_iota(jnp.int32, logf.shape, 1)
    lf_ref[...] = jnp.where(lane < n_heads, logf, 0.0)


def _kv_project(x, streams, g, w_kv, w_fg_pad, b_fg_pad, n_heads):
    n, d = x.shape
    hd = w_kv.shape[1] // 2
    nb = n // TOKEN_TILE
    rows = tuple(s * t for s, t in streams)
    shapes = [(s, t) if t % TOKEN_TILE == 0 else (1, s * t) for s, t in streams]
    row = lambda width: pl.BlockSpec((TOKEN_TILE, width), lambda i: (i, 0))

    def transposed_specs():
        specs, start = [], 0
        for (_, cols), nblk in zip(shapes, _slab_blocks(rows)):
            def index(i, start=start, nblk=nblk, nt=cols // TOKEN_TILE):
                blk = jnp.clip(i - start, 0, nblk - 1)
                return (blk // nt, 0, blk % nt)
            mode = dict(pipeline_mode=pl.Buffered(1)) if nblk == 1 else {}
            specs.append(pl.BlockSpec((None, hd, TOKEN_TILE), index, **mode))
            start += nblk
        return specs

    transposed_shapes = [jax.ShapeDtypeStruct((s, hd, cols), _F32) for s, cols in shapes]
    out = pl.pallas_call(
        functools.partial(_kv_body, rows=rows, hd=hd, n_heads=n_heads),
        grid=(nb,),
        in_specs=[row(d), _const_spec((1, d)), _const_spec(w_kv.shape), _const_spec(w_fg_pad.shape),
                  _const_spec((1, LANES))],
        out_specs=transposed_specs() + transposed_specs() + [
            row(LANES), row(hd), pl.BlockSpec((None, hd, TOKEN_TILE), lambda i: (i, 0, 0))],
        out_shape=transposed_shapes + transposed_shapes + [
            jax.ShapeDtypeStruct((n, LANES), _F32), jax.ShapeDtypeStruct((n, hd), _BF16),
            jax.ShapeDtypeStruct((nb, hd, TOKEN_TILE), _BF16)],
        compiler_params=_params("arbitrary"),
        name="kv_project",
    )(x, g.reshape(1, d), w_kv, w_fg_pad, b_fg_pad)
    ns = len(rows)
    return out[:ns], out[ns:2 * ns], out[2 * ns], out[2 * ns + 1], out[2 * ns + 2]


def _cache_body(ckt_ref, cvt_ref, kn_ref, vn_ref, k_ref, vt_ref, *, n_cache_blocks, dec_seq, new_row0):
    b = pl.program_id(0)
    j = pl.program_id(1)

    @pl.when(j < n_cache_blocks)
    def _():
        k_ref[...] = ckt_ref[...].T.astype(_BF16)
        vt_ref[...] = cvt_ref[...].astype(_BF16)

    @pl.when(j >= n_cache_blocks)
    def _():
        hd = k_ref.shape[1]
        pad = jnp.zeros((ATT_TK - dec_seq, hd), _F32)
        k_ref[...] = jnp.concatenate([kn_ref[...].astype(_F32), pad], axis=0).astype(_BF16)
        off = (new_row0 + b * dec_seq) % TOKEN_TILE
        r = lax.broadcasted_iota(jnp.int32, (TOKEN_TILE, ATT_TK), 0)
        c = lax.broadcasted_iota(jnp.int32, (TOKEN_TILE, ATT_TK), 1)
        select = jnp.where((r == off + c) & (c < dec_seq), 1.0, 0.0).astype(_BF16)
        vt_ref[...] = _dot(vn_ref[...], select).astype(_BF16)


def _cache_keys(cache_kt, cache_vt, k_new_bf, vt_new, *, new_row0, dec_seq):
    s, hd, past = cache_kt.shape
    assert past % ATT_TK == 0 and dec_seq % 16 == 0 and TOKEN_TILE % dec_seq == 0 and new_row0 % dec_seq == 0
    ncb = past // ATT_TK
    nkb = ncb + 1
    cache = pl.BlockSpec((None, hd, ATT_TK), lambda b, j: (b, 0, jnp.minimum(j, ncb - 1)))
    return pl.pallas_call(
        functools.partial(_cache_body, n_cache_blocks=ncb, dec_seq=dec_seq, new_row0=new_row0),
        grid=(s, nkb),
        in_specs=[cache, cache,
                  pl.BlockSpec((dec_seq, hd), lambda b, j: (new_row0 // dec_seq + b, 0)),
                  pl.BlockSpec((None, hd, TOKEN_TILE), lambda b, j: ((new_row0 + b * dec_seq) // TOKEN_TILE, 0, 0))],
        out_specs=[pl.BlockSpec((ATT_TK, hd), lambda b, j: (b * nkb + j, 0)),
                   pl.BlockSpec((None, hd, ATT_TK), lambda b, j: (b * nkb + j, 0, 0))],
        out_shape=[jax.ShapeDtypeStruct((s * nkb * ATT_TK, hd), _BF16),
                   jax.ShapeDtypeStruct((s * nkb, hd, ATT_TK), _BF16)],
        compiler_params=_params("arbitrary", "arbitrary"),
        name="cache_keys",
    )(cache_kt, cache_vt, k_new_bf, vt_new)


def _decay_body(lf_ref, e_ref, c_ref, *, n_heads):
    n_chunks = lf_ref.shape[0] // CUMSUM_CHUNK
    r = lax.broadcasted_iota(jnp.int32, (CUMSUM_CHUNK, CUMSUM_CHUNK), 0)
    c = lax.broadcasted_iota(jnp.int32, (CUMSUM_CHUNK, CUMSUM_CHUNK), 1)
    tril = jnp.where(r >= c, 1.0, 0.0).astype(_BF16)
    sr = lax.broadcasted_iota(jnp.int32, (LANES, LANES), 0)
    sc = lax.broadcasted_iota(jnp.int32, (LANES, LANES), 1)
    place = [jnp.where((sc == 3 * sr + j) & (sr < n_heads), 1.0, 0.0).astype(_BF16) for j in range(3)]

    for i in range(n_chunks):
        rows = slice(i * CUMSUM_CHUNK, (i + 1) * CUMSUM_CHUNK)
        hi, mid, lo = _split3(lf_ref[rows, :])
        c_ref[rows, :] = _dot(tril, hi) + _dot(tril, mid) + _dot(tril, lo)
    carry = jnp.zeros((1, LANES), _F32)
    for i in range(n_chunks):
        rows = slice(i * CUMSUM_CHUNK, (i + 1) * CUMSUM_CHUNK)
        csum = c_ref[rows, :] + carry
        carry = carry + c_ref[(i + 1) * CUMSUM_CHUNK - 1:(i + 1) * CUMSUM_CHUNK, :]
        p0, p1, p2 = _split3(-LOG2_E * csum)
        feats = _dot(p0, place[0]) + _dot(p1, place[1]) + _dot(p2, place[2])
        e_ref[rows, :] = feats.astype(_BF16)


def _decay_features(logf, n_heads):
    b, t, _ = logf.shape
    assert t % CUMSUM_CHUNK == 0 and 3 * n_heads <= LANES
    blk = pl.BlockSpec((None, t, LANES), lambda i: (i, 0, 0))
    return pl.pallas_call(
        functools.partial(_decay_body, n_heads=n_heads),
        grid=(b,),
        in_specs=[blk],
        out_specs=blk,
        out_shape=jax.ShapeDtypeStruct((b, t, LANES), _BF16),
        scratch_shapes=[pltpu.VMEM((t, LANES), _F32)],
        compiler_params=_params("arbitrary"),
        name="decay_features",
    )(logf)


def _attn_group(n_sub, q_off):
    return ATT_TK // ATT_TQ if (n_sub * ATT_TQ) % ATT_TK == 0 and q_off % ATT_TK == 0 else 1


def _attn_body(q_ref, k_ref, e_ref, vt_ref, o_ref, s_ref, *, n_sub, q_off, head_dim):
    pair = pl.program_id(1)
    qblk = pl.program_id(2)
    heads_per_block = LANES // head_dim
    group = _attn_group(n_sub, q_off)
    lane = lax.broadcasted_iota(jnp.int32, (ATT_TQ, LANES), 1)
    feat = lax.broadcasted_iota(jnp.int32, (LANES, ATT_TQ), 0)
    key_iota = lax.broadcasted_iota(jnp.int32, (ATT_TK, ATT_TQ), 0)
    qry_iota = lax.broadcasted_iota(jnp.int32, (ATT_TK, ATT_TQ), 1)
    chains = [(hh, s) for hh in range(heads_per_block) for s in range(group)]
    ones_rows = jnp.ones((SUM_ROWS, ATT_TK), _BF16)

    for g in range(n_sub // group):
        q0 = q_off + (qblk * n_sub + g * group) * ATT_TQ
        n_full = q0 // ATT_TK
        q_augs = []
        for hh, s in chains:
            head = pair * heads_per_block + hh
            bias_sel = jnp.where((feat >= 3 * head) & (feat < 3 * head + 3), 1.0, 0.0).astype(_BF16)
            head_lanes = (lane >= hh * head_dim) & (lane < (hh + 1) * head_dim)
            q = q_ref[(g * group + s) * ATT_TQ:(g * group + s + 1) * ATT_TQ, :].astype(_F32)
            q_t = jnp.where(head_lanes, q, 0.0).T.astype(_BF16)
            q_augs.append(jnp.concatenate([q_t, bias_sel], axis=0))

        def scores(j, slot, masked):
            rows = pl.ds(pl.multiple_of(j * ATT_TK, ATT_TK), ATT_TK)
            k_aug = jnp.concatenate([k_ref[rows, :], e_ref[rows, :]], axis=1)
            tiles = [_dot(k_aug, q_aug) for q_aug in q_augs]
            maxima = []
            for c, ((hh, s), sc) in enumerate(zip(chains, tiles)):
                if masked:
                    sc = jnp.where(key_iota + j * ATT_TK <= qry_iota + (q0 + s * ATT_TQ), sc, MASK_VALUE)
                s_ref[slot, c] = sc
                maxima.append(jnp.max(sc, axis=0, keepdims=True))
            return tuple(maxima)

        def consume(j, slot, maxima, state):
            out = []
            for c, ((hh, s), bm, (m, acc)) in enumerate(zip(chains, maxima, state)):
                m_new = jnp.maximum(m, bm)
                alpha = jnp.exp2(m - m_new)
                p = jnp.exp2(s_ref[slot, c] - m_new).astype(_BF16)
                vt = jnp.concatenate([vt_ref[j, hh * head_dim:(hh + 1) * head_dim, :], ones_rows], axis=0)
                out.append((m_new, alpha * acc + _dot(vt, p)))
            return tuple(out)

        def double_trip(t, carry):
            maxima, state = carry
            j = 2 * t
            prev = jnp.where(t == 0, n_full, j - 1)
            maxima_a = scores(j, 1, False)
            state = consume(prev, 0, maxima, state)
            maxima_b = scores(j + 1, 0, False)
            state = consume(j, 1, maxima_a, state)
            return maxima_b, state

        def odd_tail(carry):
            maxima, state = carry
            prev = jnp.where(n_full == 1, n_full, n_full - 2)
            maxima_a = scores(n_full - 1, 1, False)
            state = consume(prev, 0, maxima, state)
            return consume(n_full - 1, 1, maxima_a, state)

        def even_tail(carry):
            maxima, state = carry
            last = jnp.where(n_full == 0, n_full, n_full - 1)
            return consume(last, 0, maxima, state)

        init = (jnp.full((1, ATT_TQ), MASK_VALUE, _F32), jnp.zeros((head_dim + SUM_ROWS, ATT_TQ), _F32))
        carry = (scores(n_full, 0, True), (init,) * len(chains))
        carry = lax.fori_loop(0, n_full // 2, double_trip, carry)
        final = lax.cond(n_full % 2 == 1, odd_tail, even_tail, carry)
        for (hh, s), (_, acc) in zip(chains, final):
            cols = slice((g * group + s) * ATT_TQ, (g * group + s + 1) * ATT_TQ)
            out = acc[0:head_dim, :] / acc[head_dim:head_dim + 1, :]
            o_ref[hh * head_dim:(hh + 1) * head_dim, cols] = out.astype(_BF16)


def _attention(q, k, e, vt, *, n_streams, t_q, t_k, q_off, q_row0, k_row0, head_dim):
    hd = q.shape[1]
    n_pairs = hd // LANES
    tq_blk = min(t_q, 4 * ATT_TQ)
    assert t_q % tq_blk == 0 and tq_blk % ATT_TQ == 0 and t_k % ATT_TK == 0 and q_off % ATT_TQ == 0
    assert q_row0 % tq_blk == 0 and k_row0 % t_k == 0 and q_off + t_q <= t_k
    nq = t_q // tq_blk
    qb0, kb0, vb0 = q_row0 // tq_blk, k_row0 // t_k, k_row0 // ATT_TK
    nkb = t_k // ATT_TK
    assert vb0 % nkb == 0
    n_sub = tq_blk // ATT_TQ
    n_chains = (LANES // head_dim) * _attn_group(n_sub, q_off)
    return pl.pallas_call(
        functools.partial(_attn_body, n_sub=n_sub, q_off=q_off, head_dim=head_dim),
        grid=(n_streams, n_pairs, nq),
        in_specs=[
            pl.BlockSpec((tq_blk, LANES), lambda b, p, i: (qb0 + b * nq + i, p)),
            pl.BlockSpec((t_k, LANES), lambda b, p, i: (kb0 + b, p)),
            pl.BlockSpec((None, t_k, LANES), lambda b, p, i: (b, 0, 0)),
            pl.BlockSpec((nkb, LANES, ATT_TK), lambda b, p, i: (vb0 // nkb + b, p, 0)),
        ],
        out_specs=pl.BlockSpec((LANES, tq_blk), lambda b, p, i: (p, b * nq + i)),
        out_shape=jax.ShapeDtypeStruct((hd, n_streams * t_q), _BF16),
        scratch_shapes=[pltpu.VMEM((2, n_chains, ATT_TK, ATT_TQ), _F32)],
        compiler_params=_params("arbitrary", "arbitrary", "arbitrary"),
        name="forget_attention",
    )(q, k, e, vt)


def kernel(x_prompt, x_sample, cache_pool, cache_k, cache_v, cache_logf, ln_ffn1, ln_mix, ln_ffn2, w_ffn_in, w_ffn_out, w_pool, pool_scale, ln_kv, w_kv, w_fgate, b_fgate, w_q, w_o, ln_final):
    batch, seq, d = x_prompt.shape
    dec_batch, dec_seq, _ = x_sample.shape
    past_len, n_heads, head_dim = cache_k.shape[1:]
    hd = n_heads * head_dim
    depth = ln_ffn1.shape[0]
    n_a = w_pool.shape[0]
    n_prompt, n_sample = batch * seq, dec_batch * dec_seq
    rows = (n_prompt, n_sample)
    assert head_dim * 2 == LANES and seq % TOKEN_TILE == 0 and TOKEN_TILE == ATT_TK

    win = w_ffn_in.astype(_BF16)
    wout = w_ffn_out.astype(_BF16)
    wpool = w_pool.astype(_BF16)
    wkv = w_kv.astype(_BF16)
    wq = w_q.astype(_BF16)
    wo = w_o.astype(_BF16)
    wfg = jnp.pad(w_fgate, ((0, 0), (0, LANES - n_heads))).astype(_BF16)
    bfg = jnp.pad(b_fgate, (0, LANES - n_heads)).reshape(1, LANES)

    def time_minor(cache):
        return cache.transpose(0, 2, 3, 1).reshape(cache.shape[0], hd, cache.shape[1])

    def time_major(x_t, n_streams, t_len):
        g = x_t.shape[0]
        x_t = x_t.reshape(g, n_heads, head_dim, n_streams // g, t_len)
        return x_t.transpose(0, 3, 4, 1, 2).reshape(n_streams, t_len, n_heads, head_dim)

    sample_q = -(-dec_seq // ATT_TQ) * ATT_TQ
    sample_keys = past_len + ATT_TK
    assert sample_q <= ATT_TK

    slabs = [x_prompt.reshape(n_prompt, d), x_sample.reshape(n_sample, d)]
    new_pool_prompt, new_pool_sample = [], []
    ks = vs = logf128 = k_bf = vt = e_prompt = e_sample = k_bf_sample = vt_sample = None
    for l in range(depth):
        last = l == depth - 1
        attn = None
        if l < n_a:
            x = _ffn(slabs, ln_ffn1[l], win, wout, (l, 0))
            hist_p = jnp.zeros((batch, POOL_HIST_ROWS, d), _F32)
            hist_s = jnp.pad(cache_pool[l], ((0, 0), (POOL_HIST_ROWS - POOL_HIST, 0), (0, 0)))
            xp, pool_p = _pool_mix(x, hist_p, ln_mix[l], wpool[l], pool_scale[l],
                                   n_streams=batch, t_len=seq, row0=0, pos0=0)
            xs, pool_s = _pool_mix(x, hist_s, ln_mix[l], wpool[l], pool_scale[l],
                                   n_streams=dec_batch, t_len=dec_seq, row0=n_prompt, pos0=past_len)
            slabs = [xp, xs]
            new_pool_prompt.append(pool_p[:, POOL_HIST_ROWS - POOL_HIST:])
            new_pool_sample.append(pool_s[:, POOL_HIST_ROWS - POOL_HIST:])
        else:
            j = l - n_a
            x, q_bf = _ffn(slabs, ln_ffn1[l], win, wout, (l, 0),
                           query=(ln_mix[l], wq[j], head_dim ** -0.5 * LOG2_E))
            o_prompt = _attention(q_bf, k_bf, e_prompt, vt, n_streams=batch, t_q=seq, t_k=seq, q_off=0,
                                  q_row0=0, k_row0=0, head_dim=head_dim)
            q_s = jnp.pad(q_bf[n_prompt:].reshape(dec_batch, dec_seq, hd),
                          ((0, 0), (0, sample_q - dec_seq), (0, 0))).reshape(dec_batch * sample_q, hd)
            o_sample = _attention(q_s, k_bf_sample, e_sample, vt_sample, n_streams=dec_batch, t_q=sample_q,
                                  t_k=sample_keys, q_off=past_len, q_row0=0, k_row0=0, head_dim=head_dim)
            o_sample = o_sample.reshape(hd, dec_batch, sample_q)[:, :, :dec_seq].reshape(hd, n_sample)
            slabs, attn = [x], ([o_prompt, o_sample], wo[j])
        x = _ffn(slabs, ln_ffn2[l], win, wout, (l, 1), attn=attn, g_final=ln_final if last else None,
                 out_rows=rows if last else None)
        slabs = x if last else [x]
        if l == n_a - 1:
            ks, vs, logf128, k_bf, vt = _kv_project(x, [(batch, seq), (dec_batch, dec_seq)], ln_kv, wkv,
                                                    wfg, bfg, n_heads)
            e_prompt = _decay_features(logf128[:n_prompt].reshape(batch, seq, LANES), n_heads)
            logf_s = jnp.concatenate([
                jnp.pad(cache_logf, ((0, 0), (0, 0), (0, LANES - n_heads))),
                logf128[n_prompt:].reshape(dec_batch, dec_seq, LANES),
                jnp.zeros((dec_batch, ATT_TK - dec_seq, LANES), _F32)], axis=1)
            e_sample = _decay_features(logf_s, n_heads)
            k_bf_sample, vt_sample = _cache_keys(
                time_minor(cache_k), time_minor(cache_v), k_bf, vt, new_row0=n_prompt, dec_seq=dec_seq)

    y_prompt, y_sample = slabs
    logf = logf128[:, :n_heads]
    return (y_prompt.reshape(batch, seq, d), y_sample.reshape(dec_batch, dec_seq, d),
            jnp.stack(new_pool_prompt), jnp.stack(new_pool_sample),
            time_major(ks[0], batch, seq), time_major(vs[0], batch, seq),
            logf[:n_prompt].reshape(batch, seq, n_heads),
            time_major(ks[1], dec_batch, dec_seq), time_major(vs[1], dec_batch, dec_seq),
            logf[n_prompt:].reshape(dec_batch, dec_seq, n_heads))
```

```python
import functools

import jax
import jax.numpy as jnp
from jax import lax
from jax.experimental import pallas as pl
from jax.experimental.pallas import tpu as pltpu

RMS_EPS = 1e-6
POOL_WINDOWS = (2, 4, 8, 16)
POOL_HIST = max(POOL_WINDOWS) - 1

LANES = 128
TOKEN_TILE = 512
FFN_CHUNK = 256
POOL_PAD = 8
POOL_HIST_ROWS = 16
CUMSUM_CHUNK = 256
ATT_TQ = 256
ATT_TK = 512
SUM_ROWS = 16
MASK_VALUE = -1e30
LOG2_E = 1.4426950408889634
VMEM_LIMIT = 52 * 1024 * 1024

_BF16 = jnp.bfloat16
_F32 = jnp.float32


def _rms(x, g):
    ms = jnp.mean(x * x, axis=-1, keepdims=True)
    return x * lax.rsqrt(ms + RMS_EPS) * g


def _dot(a, b):
    return jnp.dot(a, b, preferred_element_type=_F32)


def _split3(x):
    hi = x.astype(_BF16)
    r = x - hi.astype(_F32)
    mid = r.astype(_BF16)
    lo = (r - mid.astype(_F32)).astype(_BF16)
    return hi, mid, lo


def _params(*sem):
    return pltpu.CompilerParams(dimension_semantics=sem, vmem_limit_bytes=VMEM_LIMIT)


def _const_spec(shape):
    zeros = (0,) * len(shape)
    return pl.BlockSpec(shape, lambda *_: zeros, pipeline_mode=pl.Buffered(1))


def _slab_blocks(rows):
    assert all(n % TOKEN_TILE == 0 for n in rows)
    return [n // TOKEN_TILE for n in rows]


def _slab_specs(rows, width, transposed=False):
    tail = width if isinstance(width, tuple) else (width,)
    specs, start = [], 0
    for nb in _slab_blocks(rows):
        def index(i, start=start, nb=nb):
            blk = jnp.clip(i - start, 0, nb - 1)
            return (0, blk) if transposed else (blk,) + (0,) * len(tail)
        shape = tail + (TOKEN_TILE,) if transposed else (TOKEN_TILE,) + tail
        mode = dict(pipeline_mode=pl.Buffered(1)) if nb == 1 else {}
        specs.append(pl.BlockSpec(shape, index, **mode))
        start += nb
    return specs


def _slab_read(refs, rows):
    i = pl.program_id(0)
    blocks = _slab_blocks(rows)
    value, end = refs[-1][...], sum(blocks)
    for ref, nb in zip(refs[-2::-1], blocks[:0:-1]):
        end -= nb
        value = jnp.where(i < end, ref[...], value)
    return value


def _slab_write(refs, rows, value):
    i = pl.program_id(0)
    start = 0
    for ref, nb in zip(refs, _slab_blocks(rows)):
        @pl.when((i >= start) & (i < start + nb))
        def _(ref=ref):
            ref[...] = value
        start += nb


def _slab_shapes(rows, width, dtype):
    tail = width if isinstance(width, tuple) else (width,)
    return [jax.ShapeDtypeStruct((n,) + tail, dtype) for n in rows]


def _ffn_body(*refs, rows_in, rows_attn, rows_out, d_ff, final_norm, query_scale):
    refs = list(refs)
    take = lambda n: [refs.pop(0) for _ in range(n)]
    x_refs = take(len(rows_in))
    if rows_attn:
        ot_refs, (wo_ref,) = take(len(rows_attn)), take(1)
    g_ref, win_ref, wout_ref = take(3)
    if final_norm:
        (gf_ref,) = take(1)
    if query_scale is not None:
        gq_ref, wq_ref = take(2)
    o_refs = take(len(rows_out))
    if query_scale is not None:
        (q_ref,) = take(1)
    (acc_ref,) = refs

    x = _slab_read(x_refs, rows_in)
    if rows_attn:
        o_t = _slab_read(ot_refs, rows_attn)
        x = x + lax.dot_general(o_t, wo_ref[...], (((0,), (0,)), ((), ())), preferred_element_type=_F32)
    xn = _rms(x, g_ref[...]).astype(_BF16)
    for c in range(d_ff // FFN_CHUNK):
        lo, hi = c * FFN_CHUNK, (c + 1) * FFN_CHUNK
        g = _dot(xn, win_ref[:, lo:hi])
        u = _dot(xn, win_ref[:, d_ff + lo:d_ff + hi])
        a = (g * (1.0 / (1.0 + jnp.exp(-g))) * u).astype(_BF16)
        part = _dot(a, wout_ref[lo:hi, :])
        if c == 0:
            acc_ref[...] = part
        else:
            acc_ref[...] += part
    y = x + 0.5 * acc_ref[...]
    if query_scale is not None:
        u_q = _rms(y, gq_ref[...]).astype(_BF16)
        q_ref[...] = (_dot(u_q, wq_ref[...]) * query_scale).astype(_BF16)
    if final_norm:
        y = _rms(y, gf_ref[...])
    _slab_write(o_refs, rows_out, y)


def _ffn(xs, g, w_in_all, w_out_all, which, *, attn=None, query=None, g_final=None, out_rows=None):
    d = xs[0].shape[1]
    d_ff = w_out_all.shape[2]
    assert d_ff % FFN_CHUNK == 0
    rows_in = tuple(x.shape[0] for x in xs)
    rows_out = tuple(out_rows) if out_rows else (sum(rows_in),)
    assert sum(rows_out) == sum(rows_in)
    n = sum(rows_in)
    picked = lambda w: pl.BlockSpec((None, None) + w.shape[2:], lambda i: tuple(which) + (0, 0),
                                    pipeline_mode=pl.Buffered(1))
    in_specs, args, rows_attn = _slab_specs(rows_in, d), list(xs), ()
    if attn is not None:
        o_ts, w_o = attn
        rows_attn = tuple(o.shape[1] for o in o_ts)
        assert sum(rows_attn) == n
        in_specs += _slab_specs(rows_attn, w_o.shape[0], transposed=True) + [_const_spec(w_o.shape)]
        args += list(o_ts) + [w_o]
    in_specs += [_const_spec((1, d)), picked(w_in_all), picked(w_out_all)]
    args += [g.reshape(1, d), w_in_all, w_out_all]
    if g_final is not None:
        in_specs.append(_const_spec((1, d)))
        args.append(g_final.reshape(1, d))
    out_specs, out_shape = _slab_specs(rows_out, d), _slab_shapes(rows_out, d, _F32)
    if query is not None:
        g_q, w_q, scale = query
        in_specs += [_const_spec((1, d)), _const_spec(w_q.shape)]
        args += [g_q.reshape(1, d), w_q]
        out_specs = out_specs + [pl.BlockSpec((TOKEN_TILE, w_q.shape[1]), lambda i: (i, 0))]
        out_shape = out_shape + [jax.ShapeDtypeStruct((n, w_q.shape[1]), _BF16)]
    out = pl.pallas_call(
        functools.partial(_ffn_body, rows_in=rows_in, rows_attn=rows_attn, rows_out=rows_out, d_ff=d_ff,
                          final_norm=g_final is not None, query_scale=None if query is None else query[2]),
        grid=(n // TOKEN_TILE,),
        in_specs=in_specs,
        out_specs=out_specs,
        out_shape=out_shape,
        scratch_shapes=[pltpu.VMEM((TOKEN_TILE, d), _F32)],
        compiler_params=_params("arbitrary"),
        name="ffn",
    )(*args)
    y = out[:len(rows_out)] if out_rows else out[0]
    return (y, out[-1]) if query is not None else y


def _pool_body(x_ref, hist_ref, g_ref, wp_ref, sc_ref, o_ref, pool_ref, buf, s2, s4, s8, *, tt, pos0):
    t = pl.program_id(1)
    d = x_ref.shape[-1]
    gw = d // len(POOL_WINDOWS)
    off = POOL_PAD + POOL_HIST_ROWS
    end = off + tt

    @pl.when(t == 0)
    def _():
        zeros = jnp.zeros((POOL_PAD, d), _F32)
        buf[0:POOL_PAD, :] = zeros
        s2[0:POOL_PAD, :] = zeros
        s4[0:POOL_PAD, :] = zeros
        buf[POOL_PAD:off, :] = hist_ref[...]

    @pl.when(t > 0)
    def _():
        buf[POOL_PAD:off, :] = buf[end - POOL_HIST_ROWS:end, :]

    x = x_ref[...]
    u = _rms(x, g_ref[...])
    buf[off:end, :] = u

    s2[POOL_PAD:end, :] = buf[POOL_PAD:end, :] + buf[POOL_PAD - 1:end - 1, :]
    s4[POOL_PAD:end, gw:] = s2[POOL_PAD:end, gw:] + s2[POOL_PAD - 2:end - 2, gw:]
    s8[POOL_PAD:end, 2 * gw:] = s4[POOL_PAD:end, 2 * gw:] + s4[POOL_PAD - 4:end - 4, 2 * gw:]
    sums = (
        s2[off:end, 0:gw],
        s4[off:end, gw:2 * gw],
        s8[off:end, 2 * gw:3 * gw],
        s8[off:end, 3 * gw:] + s8[off - 8:end - 8, 3 * gw:],
    )
    pos = pos0 + t * tt + lax.broadcasted_iota(jnp.int32, (tt, 1), 0)
    for gi, w in enumerate(POOL_WINDOWS):
        sl = slice(gi * gw, (gi + 1) * gw)
        inv_cnt = 1.0 / jnp.minimum(pos + 1, w).astype(_F32)
        diff = (sums[gi] * inv_cnt - u[:, sl]).astype(_BF16)
        y = _dot(diff, wp_ref[gi])
        o_ref[:, sl] = x[:, sl] + y * sc_ref[:, sl]

    @pl.when(t == pl.num_programs(1) - 1)
    def _():
        pool_ref[...] = buf[end - POOL_HIST_ROWS:end, :]


def _pool_mix(x_all, hist16, g, w_pool, scale, *, n_streams, t_len, row0, pos0):
    d = x_all.shape[1]
    tt = min(TOKEN_TILE, t_len)
    assert t_len % tt == 0 and row0 % tt == 0
    nt = t_len // tt
    blk0 = row0 // tt
    hist = pl.BlockSpec((None, POOL_HIST_ROWS, d), lambda b, t: (b, 0, 0))
    rows_buf = POOL_PAD + POOL_HIST_ROWS + tt
    return pl.pallas_call(
        functools.partial(_pool_body, tt=tt, pos0=pos0),
        grid=(n_streams, nt),
        in_specs=[pl.BlockSpec((tt, d), lambda b, t: (blk0 + b * nt + t, 0)), hist, _const_spec((1, d)),
                  _const_spec(w_pool.shape), _const_spec((1, d))],
        out_specs=[pl.BlockSpec((tt, d), lambda b, t: (b * nt + t, 0)), hist],
        out_shape=[jax.ShapeDtypeStruct((n_streams * t_len, d), _F32),
                   jax.ShapeDtypeStruct((n_streams, POOL_HIST_ROWS, d), _F32)],
        scratch_shapes=[pltpu.VMEM((rows_buf, d), _F32)] * 4,
        compiler_params=_params("arbitrary", "arbitrary"),
        name="pool_mix",
    )(x_all, hist16, g.reshape(1, d), w_pool, scale.reshape(1, d))


def _kv_body(x_ref, g_ref, wkv_ref, wf_ref, bf_ref, *out_refs, rows, hd, n_heads):
    ns = len(rows)
    kt_refs, vt_refs = out_refs[:ns], out_refs[ns:2 * ns]
    lf_ref, kb_ref, vtile_ref = out_refs[2 * ns:]
    xn = _rms(x_ref[...], g_ref[...]).astype(_BF16)
    kv = _dot(xn, wkv_ref[...])
    kv_t = kv.T
    _slab_write(kt_refs, rows, kv_t[0:hd, :])
    _slab_write(vt_refs, rows, kv_t[hd:2 * hd, :])
    kb_ref[...] = kv[:, 0:hd].astype(_BF16)
    vtile_ref[...] = kv_t[hd:2 * hd, :].astype(_BF16)
    z = _dot(xn, wf_ref[...]) + bf_ref[...]
    logf = jnp.minimum(z, 0.0) - jnp.log1p(jnp.exp(-jnp.abs(z)))
    lane = lax.broadcasted_iota(jnp.int32, logf.shape, 1)
    lf_ref[...] = jnp.where(lane < n_heads, logf, 0.0)


def _kv_project(x, streams, g, w_kv, w_fg_pad, b_fg_pad, n_heads):
    n, d = x.shape
    hd = w_kv.shape[1] // 2
    nb = n // TOKEN_TILE
    rows = tuple(s * t for s, t in streams)
    shapes = [(s, t) if t % TOKEN_TILE == 0 else (1, s * t) for s, t in streams]
    row = lambda width: pl.BlockSpec((TOKEN_TILE, width), lambda i: (i, 0))

    def transposed_specs():
        specs, start = [], 0
        for (_, cols), nblk in zip(shapes, _slab_blocks(rows)):
            def index(i, start=start, nblk=nblk, nt=cols // TOKEN_TILE):
                blk = jnp.clip(i - start, 0, nblk - 1)
                return (blk // nt, 0, blk % nt)
            mode = dict(pipeline_mode=pl.Buffered(1)) if nblk == 1 else {}
            specs.append(pl.BlockSpec((None, hd, TOKEN_TILE), index, **mode))
            start += nblk
        return specs

    transposed_shapes = [jax.ShapeDtypeStruct((s, hd, cols), _F32) for s, cols in shapes]
    out = pl.pallas_call(
        functools.partial(_kv_body, rows=rows, hd=hd, n_heads=n_heads),
        grid=(nb,),
        in_specs=[row(d), _const_spec((1, d)), _const_spec(w_kv.shape), _const_spec(w_fg_pad.shape),
                  _const_spec((1, LANES))],
        out_specs=transposed_specs() + transposed_specs() + [
            row(LANES), row(hd), pl.BlockSpec((None, hd, TOKEN_TILE), lambda i: (i, 0, 0))],
        out_shape=transposed_shapes + transposed_shapes + [
            jax.ShapeDtypeStruct((n, LANES), _F32), jax.ShapeDtypeStruct((n, hd), _BF16),
            jax.ShapeDtypeStruct((nb, hd, TOKEN_TILE), _BF16)],
        compiler_params=_params("arbitrary"),
        name="kv_project",
    )(x, g.reshape(1, d), w_kv, w_fg_pad, b_fg_pad)
    ns = len(rows)
    return out[:ns], out[ns:2 * ns], out[2 * ns], out[2 * ns + 1], out[2 * ns + 2]


def _cache_body(ckt_ref, cvt_ref, kn_ref, vn_ref, k_ref, vt_ref, *, n_cache_blocks, dec_seq, new_row0):
    b = pl.program_id(0)
    j = pl.program_id(1)

    @pl.when(j < n_cache_blocks)
    def _():
        k_ref[...] = ckt_ref[...].T.astype(_BF16)
        vt_ref[...] = cvt_ref[...].astype(_BF16)

    @pl.when(j >= n_cache_blocks)
    def _():
        hd = k_ref.shape[1]
        pad = jnp.zeros((ATT_TK - dec_seq, hd), _F32)
        k_ref[...] = jnp.concatenate([kn_ref[...].astype(_F32), pad], axis=0).astype(_BF16)
        off = (new_row0 + b * dec_seq) % TOKEN_TILE
        r = lax.broadcasted_iota(jnp.int32, (TOKEN_TILE, ATT_TK), 0)
        c = lax.broadcasted_iota(jnp.int32, (TOKEN_TILE, ATT_TK), 1)
        select = jnp.where((r == off + c) & (c < dec_seq), 1.0, 0.0).astype(_BF16)
        vt_ref[...] = _dot(vn_ref[...], select).astype(_BF16)


def _cache_keys(cache_kt, cache_vt, k_new_bf, vt_new, *, new_row0, dec_seq):
    s, hd, past = cache_kt.shape
    assert past % ATT_TK == 0 and dec_seq % 16 == 0 and TOKEN_TILE % dec_seq == 0 and new_row0 % dec_seq == 0
    ncb = past // ATT_TK
    nkb = ncb + 1
    cache = pl.BlockSpec((None, hd, ATT_TK), lambda b, j: (b, 0, jnp.minimum(j, ncb - 1)))
    return pl.pallas_call(
        functools.partial(_cache_body, n_cache_blocks=ncb, dec_seq=dec_seq, new_row0=new_row0),
        grid=(s, nkb),
        in_specs=[cache, cache,
                  pl.BlockSpec((dec_seq, hd), lambda b, j: (new_row0 // dec_seq + b, 0)),
                  pl.BlockSpec((None, hd, TOKEN_TILE), lambda b, j: ((new_row0 + b * dec_seq) // TOKEN_TILE, 0, 0))],
        out_specs=[pl.BlockSpec((ATT_TK, hd), lambda b, j: (b * nkb + j, 0)),
                   pl.BlockSpec((None, hd, ATT_TK), lambda b, j: (b * nkb + j, 0, 0))],
        out_shape=[jax.ShapeDtypeStruct((s * nkb * ATT_TK, hd), _BF16),
                   jax.ShapeDtypeStruct((s * nkb, hd, ATT_TK), _BF16)],
        compiler_params=_params("arbitrary", "arbitrary"),
        name="cache_keys",
    )(cache_kt, cache_vt, k_new_bf, vt_new)


def _decay_body(lf_ref, e_ref, c_ref, *, n_heads):
    n_chunks = lf_ref.shape[0] // CUMSUM_CHUNK
    r = lax.broadcasted_iota(jnp.int32, (CUMSUM_CHUNK, CUMSUM_CHUNK), 0)
    c = lax.broadcasted_iota(jnp.int32, (CUMSUM_CHUNK, CUMSUM_CHUNK), 1)
    tril = jnp.where(r >= c, 1.0, 0.0).astype(_BF16)
    sr = lax.broadcasted_iota(jnp.int32, (LANES, LANES), 0)
    sc = lax.broadcasted_iota(jnp.int32, (LANES, LANES), 1)
    place = [jnp.where((sc == 3 * sr + j) & (sr < n_heads), 1.0, 0.0).astype(_BF16) for j in range(3)]

    for i in range(n_chunks):
        rows = slice(i * CUMSUM_CHUNK, (i + 1) * CUMSUM_CHUNK)
        hi, mid, lo = _split3(lf_ref[rows, :])
        c_ref[rows, :] = _dot(tril, hi) + _dot(tril, mid) + _dot(tril, lo)
    carry = jnp.zeros((1, LANES), _F32)
    for i in range(n_chunks):
        rows = slice(i * CUMSUM_CHUNK, (i + 1) * CUMSUM_CHUNK)
        csum = c_ref[rows, :] + carry
        carry = carry + c_ref[(i + 1) * CUMSUM_CHUNK - 1:(i + 1) * CUMSUM_CHUNK, :]
        p0, p1, p2 = _split3(-LOG2_E * csum)
        feats = _dot(p0, place[0]) + _dot(p1, place[1]) + _dot(p2, place[2])
        e_ref[rows, :] = feats.astype(_BF16)


def _decay_features(logf, n_heads):
    b, t, _ = logf.shape
    assert t % CUMSUM_CHUNK == 0 and 3 * n_heads <= LANES
    blk = pl.BlockSpec((None, t, LANES), lambda i: (i, 0, 0))
    return pl.pallas_call(
        functools.partial(_decay_body, n_heads=n_heads),
        grid=(b,),
        in_specs=[blk],
        out_specs=blk,
        out_shape=jax.ShapeDtypeStruct((b, t, LANES), _BF16),
        scratch_shapes=[pltpu.VMEM((t, LANES), _F32)],
        compiler_params=_params("arbitrary"),
        name="decay_features",
    )(logf)


def _attn_group(n_sub, q_off):
    return ATT_TK // ATT_TQ if (n_sub * ATT_TQ) % ATT_TK == 0 and q_off % ATT_TK == 0 else 1


def _key_block_pipeline(n_full, scores, consume, state):
    def double_trip(t, carry):
        maxima, state = carry
        j = 2 * t
        prev = jnp.where(t == 0, n_full, j - 1)
        maxima_a = scores(j, 1, False)
        state = consume(prev, 0, maxima, state)
        maxima_b = scores(j + 1, 0, False)
        state = consume(j, 1, maxima_a, state)
        return maxima_b, state

    def odd_tail(carry):
        maxima, state = carry
        prev = jnp.where(n_full == 1, n_full, n_full - 2)
        maxima_a = scores(n_full - 1, 1, False)
        state = consume(prev, 0, maxima, state)
        return consume(n_full - 1, 1, maxima_a, state)

    def even_tail(carry):
        maxima, state = carry
        last = jnp.where(n_full == 0, n_full, n_full - 1)
        return consume(last, 0, maxima, state)

    carry = (scores(n_full, 0, True), state)
    carry = lax.fori_loop(0, n_full // 2, double_trip, carry)
    return lax.cond(n_full % 2 == 1, odd_tail, even_tail, carry)


def _attn_body(q_ref, k_ref, e_ref, vt_ref, o_ref, s_ref, *, n_sub, q_off, head_dim):
    pair = pl.program_id(1)
    qblk = pl.program_id(2)
    heads_per_block = LANES // head_dim
    group = _attn_group(n_sub, q_off)
    lane = lax.broadcasted_iota(jnp.int32, (ATT_TQ, LANES), 1)
    feat = lax.broadcasted_iota(jnp.int32, (LANES, ATT_TQ), 0)
    key_iota = lax.broadcasted_iota(jnp.int32, (ATT_TK, ATT_TQ), 0)
    qry_iota = lax.broadcasted_iota(jnp.int32, (ATT_TK, ATT_TQ), 1)
    chains = [(hh, s) for hh in range(heads_per_block) for s in range(group)]
    ones_rows = jnp.ones((SUM_ROWS, ATT_TK), _BF16)

    for g in range(n_sub // group):
        q0 = q_off + (qblk * n_sub + g * group) * ATT_TQ
        q_augs = []
        for hh, s in chains:
            head = pair * heads_per_block + hh
            bias_sel = jnp.where((feat >= 3 * head) & (feat < 3 * head + 3), 1.0, 0.0).astype(_BF16)
            head_lanes = (lane >= hh * head_dim) & (lane < (hh + 1) * head_dim)
            q = q_ref[(g * group + s) * ATT_TQ:(g * group + s + 1) * ATT_TQ, :].astype(_F32)
            q_t = jnp.where(head_lanes, q, 0.0).T.astype(_BF16)
            q_augs.append(jnp.concatenate([q_t, bias_sel], axis=0))

        def scores(j, slot, masked):
            rows = pl.ds(pl.multiple_of(j * ATT_TK, ATT_TK), ATT_TK)
            k_aug = jnp.concatenate([k_ref[rows, :], e_ref[rows, :]], axis=1)
            tiles = [_dot(k_aug, q_aug) for q_aug in q_augs]
            maxima = []
            for c, ((hh, s), sc) in enumerate(zip(chains, tiles)):
                if masked:
                    sc = jnp.where(key_iota + j * ATT_TK <= qry_iota + (q0 + s * ATT_TQ), sc, MASK_VALUE)
                s_ref[slot, c] = sc
                maxima.append(jnp.max(sc, axis=0, keepdims=True))
            return tuple(maxima)

        def consume(j, slot, maxima, state):
            out = []
            for c, ((hh, s), bm, (m, acc)) in enumerate(zip(chains, maxima, state)):
                m_new = jnp.maximum(m, bm)
                alpha = jnp.exp2(m - m_new)
                p = jnp.exp2(s_ref[slot, c] - m_new).astype(_BF16)
                vt = jnp.concatenate([vt_ref[j, hh * head_dim:(hh + 1) * head_dim, :], ones_rows], axis=0)
                out.append((m_new, alpha * acc + _dot(vt, p)))
            return tuple(out)

        init = (jnp.full((1, ATT_TQ), MASK_VALUE, _F32), jnp.zeros((head_dim + SUM_ROWS, ATT_TQ), _F32))
        final = _key_block_pipeline(q0 // ATT_TK, scores, consume, (init,) * len(chains))
        for (hh, s), (_, acc) in zip(chains, final):
            cols = slice((g * group + s) * ATT_TQ, (g * group + s + 1) * ATT_TQ)
            out = acc[0:head_dim, :] / acc[head_dim:head_dim + 1, :]
            o_ref[hh * head_dim:(hh + 1) * head_dim, cols] = out.astype(_BF16)


def _attn_packed_body(q_ref, k_ref, e_ref, vt_ref, o_ref, s_ref, *, pairs, q_off, head_dim):
    pair0 = pl.program_id(1) * pairs
    half = ATT_TQ // 2
    q0 = q_off
    feat = lax.broadcasted_iota(jnp.int32, (LANES, ATT_TQ), 0)
    col_head = lax.broadcasted_iota(jnp.int32, (LANES, ATT_TQ), 1) // half
    key_iota = lax.broadcasted_iota(jnp.int32, (ATT_TK, ATT_TQ), 0)
    qry_iota = lax.broadcasted_iota(jnp.int32, (ATT_TK, ATT_TQ), 1) % half
    ones_rows = jnp.ones((SUM_ROWS, ATT_TK), _BF16)

    q_augs = []
    for pp in range(pairs):
        q_t = q_ref[:, pp * LANES:(pp + 1) * LANES].astype(_F32).T
        q_t = jnp.concatenate([q_t, q_t], axis=1)
        q_t = jnp.where(feat // head_dim == col_head, q_t, 0.0).astype(_BF16)
        head = (pair0 + pp) * (LANES // head_dim) + col_head
        bias_sel = jnp.where((feat >= 3 * head) & (feat < 3 * head + 3), 1.0, 0.0).astype(_BF16)
        q_augs.append(jnp.concatenate([q_t, bias_sel], axis=0))

    def scores(j, slot, masked):
        rows = pl.ds(pl.multiple_of(j * ATT_TK, ATT_TK), ATT_TK)
        tiles = [_dot(jnp.concatenate([k_ref[rows, pp * LANES:(pp + 1) * LANES], e_ref[rows, :]], axis=1),
                      q_augs[pp]) for pp in range(pairs)]
        maxima = []
        for pp, sc in enumerate(tiles):
            if masked:
                sc = jnp.where(key_iota + j * ATT_TK <= qry_iota + q0, sc, MASK_VALUE)
            s_ref[slot, pp] = sc
            maxima.append(jnp.max(sc, axis=0, keepdims=True))
        return tuple(maxima)

    def consume(j, slot, maxima, state):
        out = []
        for pp, (bm, (m, acc)) in enumerate(zip(maxima, state)):
            m_new = jnp.maximum(m, bm)
            alpha = jnp.exp2(m - m_new)
            p = jnp.exp2(s_ref[slot, pp] - m_new).astype(_BF16)
            vt = jnp.concatenate([vt_ref[j, pp * LANES:(pp + 1) * LANES, :], ones_rows], axis=0)
            out.append((m_new, alpha * acc + _dot(vt, p)))
        return tuple(out)

    init = (jnp.full((1, ATT_TQ), MASK_VALUE, _F32), jnp.zeros((LANES + SUM_ROWS, ATT_TQ), _F32))
    final = _key_block_pipeline(jnp.int32(q0 // ATT_TK), scores, consume, (init,) * pairs)
    for pp, (_, acc) in enumerate(final):
        for hh in range(LANES // head_dim):
            cols = slice(hh * half, (hh + 1) * half)
            out = acc[hh * head_dim:(hh + 1) * head_dim, cols] / acc[LANES:LANES + 1, cols]
            o_ref[pp * LANES + hh * head_dim:pp * LANES + (hh + 1) * head_dim, :] = out.astype(_BF16)


def _attention(q, k, e, vt, *, n_streams, t_q, t_k, q_off, q_row0, k_row0, head_dim, packed_pairs=0):
    hd = q.shape[1]
    n_pairs = hd // LANES
    nkb = t_k // ATT_TK
    kb0, vb0 = k_row0 // t_k, k_row0 // ATT_TK
    assert t_k % ATT_TK == 0 and k_row0 % t_k == 0 and vb0 % nkb == 0 and q_off + t_q <= t_k
    if packed_pairs:
        width = packed_pairs * LANES
        assert t_q == ATT_TQ // 2 and q_off % ATT_TK == 0 and n_pairs % packed_pairs == 0 and q_row0 % t_q == 0
        body = functools.partial(_attn_packed_body, pairs=packed_pairs, q_off=q_off, head_dim=head_dim)
        grid, tq_blk, n_chains, nq = (n_streams, n_pairs // packed_pairs, 1), t_q, packed_pairs, 1
    else:
        width = LANES
        tq_blk = min(t_q, 4 * ATT_TQ)
        assert t_q % tq_blk == 0 and tq_blk % ATT_TQ == 0 and q_off % ATT_TQ == 0 and q_row0 % tq_blk == 0
        nq = t_q // tq_blk
        n_sub = tq_blk // ATT_TQ
        body = functools.partial(_attn_body, n_sub=n_sub, q_off=q_off, head_dim=head_dim)
        grid, n_chains = (n_streams, n_pairs, nq), (LANES // head_dim) * _attn_group(n_sub, q_off)
    qb0 = q_row0 // tq_blk
    return pl.pallas_call(
        body,
        grid=grid,
        in_specs=[
            pl.BlockSpec((tq_blk, width), lambda b, p, i: (qb0 + b * nq + i, p)),
            pl.BlockSpec((t_k, width), lambda b, p, i: (kb0 + b, p)),
            pl.BlockSpec((None, t_k, LANES), lambda b, p, i: (b, 0, 0)),
            pl.BlockSpec((nkb, width, ATT_TK), lambda b, p, i: (vb0 // nkb + b, p, 0)),
        ],
        out_specs=pl.BlockSpec((width, tq_blk), lambda b, p, i: (p, b * nq + i)),
        out_shape=jax.ShapeDtypeStruct((hd, n_streams * t_q), _BF16),
        scratch_shapes=[pltpu.VMEM((2, n_chains, ATT_TK, ATT_TQ), _F32)],
        compiler_params=_params("arbitrary", "arbitrary", "arbitrary"),
        name="forget_attention",
    )(q, k, e, vt)


def kernel(x_prompt, x_sample, cache_pool, cache_k, cache_v, cache_logf, ln_ffn1, ln_mix, ln_ffn2, w_ffn_in, w_ffn_out, w_pool, pool_scale, ln_kv, w_kv, w_fgate, b_fgate, w_q, w_o, ln_final):
    batch, seq, d = x_prompt.shape
    dec_batch, dec_seq, _ = x_sample.shape
    past_len, n_heads, head_dim = cache_k.shape[1:]
    hd = n_heads * head_dim
    depth = ln_ffn1.shape[0]
    n_a = w_pool.shape[0]
    n_prompt, n_sample = batch * seq, dec_batch * dec_seq
    rows = (n_prompt, n_sample)
    assert head_dim * 2 == LANES and seq % TOKEN_TILE == 0 and TOKEN_TILE == ATT_TK

    win = w_ffn_in.astype(_BF16)
    wout = w_ffn_out.astype(_BF16)
    wpool = w_pool.astype(_BF16)
    wkv = w_kv.astype(_BF16)
    wq = w_q.astype(_BF16)
    wo = w_o.astype(_BF16)
    wfg = jnp.pad(w_fgate, ((0, 0), (0, LANES - n_heads))).astype(_BF16)
    bfg = jnp.pad(b_fgate, (0, LANES - n_heads)).reshape(1, LANES)

    def time_minor(cache):
        return cache.transpose(0, 2, 3, 1).reshape(cache.shape[0], hd, cache.shape[1])

    def time_major(x_t, n_streams, t_len):
        g = x_t.shape[0]
        x_t = x_t.reshape(g, n_heads, head_dim, n_streams // g, t_len)
        return x_t.transpose(0, 3, 4, 1, 2).reshape(n_streams, t_len, n_heads, head_dim)

    packed_pairs = 4 if dec_seq <= ATT_TQ // 2 and (hd // LANES) % 4 == 0 else 0
    sample_q = ATT_TQ // 2 if packed_pairs else -(-dec_seq // ATT_TQ) * ATT_TQ
    sample_keys = past_len + ATT_TK
    assert sample_q <= ATT_TK

    slabs = [x_prompt.reshape(n_prompt, d), x_sample.reshape(n_sample, d)]
    new_pool_prompt, new_pool_sample = [], []
    ks = vs = logf128 = k_bf = vt = e_prompt = e_sample = k_bf_sample = vt_sample = None
    for l in range(depth):
        last = l == depth - 1
        attn = None
        if l < n_a:
            x = _ffn(slabs, ln_ffn1[l], win, wout, (l, 0))
            hist_p = jnp.zeros((batch, POOL_HIST_ROWS, d), _F32)
            hist_s = jnp.pad(cache_pool[l], ((0, 0), (POOL_HIST_ROWS - POOL_HIST, 0), (0, 0)))
            xp, pool_p = _pool_mix(x, hist_p, ln_mix[l], wpool[l], pool_scale[l],
                                   n_streams=batch, t_len=seq, row0=0, pos0=0)
            xs, pool_s = _pool_mix(x, hist_s, ln_mix[l], wpool[l], pool_scale[l],
                                   n_streams=dec_batch, t_len=dec_seq, row0=n_prompt, pos0=past_len)
            slabs = [xp, xs]
            new_pool_prompt.append(pool_p[:, POOL_HIST_ROWS - POOL_HIST:])
            new_pool_sample.append(pool_s[:, POOL_HIST_ROWS - POOL_HIST:])
        else:
            j = l - n_a
            x, q_bf = _ffn(slabs, ln_ffn1[l], win, wout, (l, 0),
                           query=(ln_mix[l], wq[j], head_dim ** -0.5 * LOG2_E))
            o_prompt = _attention(q_bf, k_bf, e_prompt, vt, n_streams=batch, t_q=seq, t_k=seq, q_off=0,
                                  q_row0=0, k_row0=0, head_dim=head_dim)
            q_s = jnp.pad(q_bf[n_prompt:].reshape(dec_batch, dec_seq, hd),
                          ((0, 0), (0, sample_q - dec_seq), (0, 0))).reshape(dec_batch * sample_q, hd)
            o_sample = _attention(q_s, k_bf_sample, e_sample, vt_sample, n_streams=dec_batch, t_q=sample_q,
                                  t_k=sample_keys, q_off=past_len, q_row0=0, k_row0=0, head_dim=head_dim,
                                  packed_pairs=packed_pairs)
            o_sample = o_sample.reshape(hd, dec_batch, sample_q)[:, :, :dec_seq].reshape(hd, n_sample)
            slabs, attn = [x], ([o_prompt, o_sample], wo[j])
        x = _ffn(slabs, ln_ffn2[l], win, wout, (l, 1), attn=attn, g_final=ln_final if last else None,
                 out_rows=rows if last else None)
        slabs = x if last else [x]
        if l == n_a - 1:
            ks, vs, logf128, k_bf, vt = _kv_project(x, [(batch, seq), (dec_batch, dec_seq)], ln_kv, wkv,
                                                    wfg, bfg, n_heads)
            e_prompt = _decay_features(logf128[:n_prompt].reshape(batch, seq, LANES), n_heads)
            logf_s = jnp.concatenate([
                jnp.pad(cache_logf, ((0, 0), (0, 0), (0, LANES - n_heads))),
                logf128[n_prompt:].reshape(dec_batch, dec_seq, LANES),
                jnp.zeros((dec_batch, ATT_TK - dec_seq, LANES), _F32)], axis=1)
            e_sample = _decay_features(logf_s, n_heads)
            k_bf_sample, vt_sample = _cache_keys(
                time_minor(cache_k), time_minor(cache_v), k_bf, vt, new_row0=n_prompt, dec_seq=dec_seq)

    y_prompt, y_sample = slabs
    logf = logf128[:, :n_heads]
    return (y_prompt.reshape(batch, seq, d), y_sample.reshape(dec_batch, dec_seq, d),
            jnp.stack(new_pool_prompt), jnp.stack(new_pool_sample),
            time_major(ks[0], batch, seq), time_major(vs[0], batch, seq),
            logf[:n_prompt].reshape(batch, seq, n_heads),
            time_major(ks[1], dec_batch, dec_seq), time_major(vs[1], dec_batch, dec_seq),
            logf[n_prompt:].reshape(dec_batch, dec_seq, n_heads))
```

```python
import functools

import jax
import jax.numpy as jnp
from jax import lax
from jax.experimental import pallas as pl
from jax.experimental.pallas import tpu as pltpu

RMS_EPS = 1e-6
POOL_WINDOWS = (2, 4, 8, 16)
POOL_HIST = max(POOL_WINDOWS) - 1

LANES = 128
TOKEN_TILE = 512
FFN_CHUNK = 256
POOL_PAD = 8
POOL_HIST_ROWS = 16
CUMSUM_CHUNK = 256
ATT_TQ = 256
ATT_TK = 512
SUM_ROWS = 16
MASK_VALUE = -1e30
LOG2_E = 1.4426950408889634
VMEM_LIMIT = 52 * 1024 * 1024

_BF16 = jnp.bfloat16
_F32 = jnp.float32


def _rms(x, g):
    ms = jnp.mean(x * x, axis=-1, keepdims=True)
    return x * lax.rsqrt(ms + RMS_EPS) * g


def _dot(a, b):
    return jnp.dot(a, b, preferred_element_type=_F32)


def _split3(x):
    hi = x.astype(_BF16)
    r = x - hi.astype(_F32)
    mid = r.astype(_BF16)
    lo = (r - mid.astype(_F32)).astype(_BF16)
    return hi, mid, lo


def _params(*sem):
    return pltpu.CompilerParams(dimension_semantics=sem, vmem_limit_bytes=VMEM_LIMIT)


def _const_spec(shape):
    zeros = (0,) * len(shape)
    return pl.BlockSpec(shape, lambda *_: zeros, pipeline_mode=pl.Buffered(1))


def _slab_blocks(rows):
    assert all(n % TOKEN_TILE == 0 for n in rows)
    return [n // TOKEN_TILE for n in rows]


def _slab_specs(rows, width, transposed=False):
    tail = width if isinstance(width, tuple) else (width,)
    specs, start = [], 0
    for nb in _slab_blocks(rows):
        def index(i, start=start, nb=nb):
            blk = jnp.clip(i - start, 0, nb - 1)
            return (0, blk) if transposed else (blk,) + (0,) * len(tail)
        shape = tail + (TOKEN_TILE,) if transposed else (TOKEN_TILE,) + tail
        mode = dict(pipeline_mode=pl.Buffered(1)) if nb == 1 else {}
        specs.append(pl.BlockSpec(shape, index, **mode))
        start += nb
    return specs


def _slab_read(refs, rows):
    i = pl.program_id(0)
    blocks = _slab_blocks(rows)
    value, end = refs[-1][...], sum(blocks)
    for ref, nb in zip(refs[-2::-1], blocks[:0:-1]):
        end -= nb
        value = jnp.where(i < end, ref[...], value)
    return value


def _slab_write(refs, rows, value):
    i = pl.program_id(0)
    start = 0
    for ref, nb in zip(refs, _slab_blocks(rows)):
        @pl.when((i >= start) & (i < start + nb))
        def _(ref=ref):
            ref[...] = value
        start += nb


def _slab_shapes(rows, width, dtype):
    tail = width if isinstance(width, tuple) else (width,)
    return [jax.ShapeDtypeStruct((n,) + tail, dtype) for n in rows]


def _ffn_body(*refs, rows_in, rows_attn, rows_out, d_ff, final_norm, query_scale):
    refs = list(refs)
    take = lambda n: [refs.pop(0) for _ in range(n)]
    x_refs = take(len(rows_in))
    if rows_attn:
        ot_refs, (wo_ref,) = take(len(rows_attn)), take(1)
    g_ref, win_ref, wout_ref = take(3)
    if final_norm:
        (gf_ref,) = take(1)
    if query_scale is not None:
        gq_ref, wq_ref = take(2)
    o_refs = take(len(rows_out))
    if query_scale is not None:
        (q_ref,) = take(1)
    (acc_ref,) = refs

    x = _slab_read(x_refs, rows_in)
    if rows_attn:
        o_t = _slab_read(ot_refs, rows_attn)
        x = x + lax.dot_general(o_t, wo_ref[...], (((0,), (0,)), ((), ())), preferred_element_type=_F32)
    xn = _rms(x, g_ref[...]).astype(_BF16)
    for c in range(d_ff // FFN_CHUNK):
        lo, hi = c * FFN_CHUNK, (c + 1) * FFN_CHUNK
        g = _dot(xn, win_ref[:, lo:hi].astype(_BF16))
        u = _dot(xn, win_ref[:, d_ff + lo:d_ff + hi].astype(_BF16))
        a = (g * (1.0 / (1.0 + jnp.exp(-g))) * u).astype(_BF16)
        part = _dot(a, wout_ref[lo:hi, :].astype(_BF16))
        if c == 0:
            acc_ref[...] = part
        else:
            acc_ref[...] += part
    y = x + 0.5 * acc_ref[...]
    if query_scale is not None:
        u_q = _rms(y, gq_ref[...]).astype(_BF16)
        q_ref[...] = (_dot(u_q, wq_ref[...]) * query_scale).astype(_BF16)
    if final_norm:
        y = _rms(y, gf_ref[...])
    _slab_write(o_refs, rows_out, y)


def _ffn(xs, g, w_in_all, w_out_all, which, *, attn=None, query=None, g_final=None, out_rows=None):
    d = xs[0].shape[1]
    d_ff = w_out_all.shape[2]
    assert d_ff % FFN_CHUNK == 0
    rows_in = tuple(x.shape[0] for x in xs)
    rows_out = tuple(out_rows) if out_rows else (sum(rows_in),)
    assert sum(rows_out) == sum(rows_in)
    n = sum(rows_in)
    picked = lambda w: pl.BlockSpec((None, None) + w.shape[2:], lambda i: tuple(which) + (0, 0),
                                    pipeline_mode=pl.Buffered(1))
    in_specs, args, rows_attn = _slab_specs(rows_in, d), list(xs), ()
    if attn is not None:
        o_ts, w_o = attn
        rows_attn = tuple(o.shape[1] for o in o_ts)
        assert sum(rows_attn) == n
        in_specs += _slab_specs(rows_attn, w_o.shape[0], transposed=True) + [_const_spec(w_o.shape)]
        args += list(o_ts) + [w_o]
    in_specs += [_const_spec((1, d)), picked(w_in_all), picked(w_out_all)]
    args += [g.reshape(1, d), w_in_all, w_out_all]
    if g_final is not None:
        in_specs.append(_const_spec((1, d)))
        args.append(g_final.reshape(1, d))
    out_specs, out_shape = _slab_specs(rows_out, d), _slab_shapes(rows_out, d, _F32)
    if query is not None:
        g_q, w_q, scale = query
        in_specs += [_const_spec((1, d)), _const_spec(w_q.shape)]
        args += [g_q.reshape(1, d), w_q]
        out_specs = out_specs + [pl.BlockSpec((TOKEN_TILE, w_q.shape[1]), lambda i: (i, 0))]
        out_shape = out_shape + [jax.ShapeDtypeStruct((n, w_q.shape[1]), _BF16)]
    out = pl.pallas_call(
        functools.partial(_ffn_body, rows_in=rows_in, rows_attn=rows_attn, rows_out=rows_out, d_ff=d_ff,
                          final_norm=g_final is not None, query_scale=None if query is None else query[2]),
        grid=(n // TOKEN_TILE,),
        in_specs=in_specs,
        out_specs=out_specs,
        out_shape=out_shape,
        scratch_shapes=[pltpu.VMEM((TOKEN_TILE, d), _F32)],
        compiler_params=_params("arbitrary"),
        name="ffn",
    )(*args)
    y = out[:len(rows_out)] if out_rows else out[0]
    return (y, out[-1]) if query is not None else y


def _pool_body(x_ref, hist_ref, g_ref, wp_ref, sc_ref, o_ref, pool_ref, buf, s2, s4, s8, *, tt, pos0):
    t = pl.program_id(1)
    d = x_ref.shape[-1]
    gw = d // len(POOL_WINDOWS)
    off = POOL_PAD + POOL_HIST_ROWS
    end = off + tt

    @pl.when(t == 0)
    def _():
        zeros = jnp.zeros((POOL_PAD, d), _F32)
        buf[0:POOL_PAD, :] = zeros
        s2[0:POOL_PAD, :] = zeros
        s4[0:POOL_PAD, :] = zeros
        buf[POOL_PAD:off, :] = hist_ref[...]

    @pl.when(t > 0)
    def _():
        buf[POOL_PAD:off, :] = buf[end - POOL_HIST_ROWS:end, :]

    x = x_ref[...]
    u = _rms(x, g_ref[...])
    buf[off:end, :] = u

    s2[POOL_PAD:end, :] = buf[POOL_PAD:end, :] + buf[POOL_PAD - 1:end - 1, :]
    s4[POOL_PAD:end, gw:] = s2[POOL_PAD:end, gw:] + s2[POOL_PAD - 2:end - 2, gw:]
    s8[POOL_PAD:end, 2 * gw:] = s4[POOL_PAD:end, 2 * gw:] + s4[POOL_PAD - 4:end - 4, 2 * gw:]
    sums = (
        s2[off:end, 0:gw],
        s4[off:end, gw:2 * gw],
        s8[off:end, 2 * gw:3 * gw],
        s8[off:end, 3 * gw:] + s8[off - 8:end - 8, 3 * gw:],
    )
    pos = pos0 + t * tt + lax.broadcasted_iota(jnp.int32, (tt, 1), 0)
    for gi, w in enumerate(POOL_WINDOWS):
        sl = slice(gi * gw, (gi + 1) * gw)
        inv_cnt = 1.0 / jnp.minimum(pos + 1, w).astype(_F32)
        diff = (sums[gi] * inv_cnt - u[:, sl]).astype(_BF16)
        y = _dot(diff, wp_ref[gi])
        o_ref[:, sl] = x[:, sl] + y * sc_ref[:, sl]

    @pl.when(t == pl.num_programs(1) - 1)
    def _():
        pool_ref[...] = buf[end - POOL_HIST_ROWS:end, :]


def _pool_mix(x_all, hist16, g, w_pool, scale, *, n_streams, t_len, row0, pos0):
    d = x_all.shape[1]
    tt = min(TOKEN_TILE, t_len)
    assert t_len % tt == 0 and row0 % tt == 0
    nt = t_len // tt
    blk0 = row0 // tt
    hist = pl.BlockSpec((None, POOL_HIST_ROWS, d), lambda b, t: (b, 0, 0))
    rows_buf = POOL_PAD + POOL_HIST_ROWS + tt
    return pl.pallas_call(
        functools.partial(_pool_body, tt=tt, pos0=pos0),
        grid=(n_streams, nt),
        in_specs=[pl.BlockSpec((tt, d), lambda b, t: (blk0 + b * nt + t, 0)), hist, _const_spec((1, d)),
                  _const_spec(w_pool.shape), _const_spec((1, d))],
        out_specs=[pl.BlockSpec((tt, d), lambda b, t: (b * nt + t, 0)), hist],
        out_shape=[jax.ShapeDtypeStruct((n_streams * t_len, d), _F32),
                   jax.ShapeDtypeStruct((n_streams, POOL_HIST_ROWS, d), _F32)],
        scratch_shapes=[pltpu.VMEM((rows_buf, d), _F32)] * 4,
        compiler_params=_params("arbitrary", "arbitrary"),
        name="pool_mix",
    )(x_all, hist16, g.reshape(1, d), w_pool, scale.reshape(1, d))


def _kv_body(x_ref, g_ref, wkv_ref, wf_ref, bf_ref, *out_refs, rows, hd, n_heads):
    ns = len(rows)
    kt_refs, vt_refs = out_refs[:ns], out_refs[ns:2 * ns]
    lf_ref, kb_ref, vtile_ref = out_refs[2 * ns:]
    xn = _rms(x_ref[...], g_ref[...]).astype(_BF16)
    kv = _dot(xn, wkv_ref[...])
    kv_t = kv.T
    _slab_write(kt_refs, rows, kv_t[0:hd, :])
    _slab_write(vt_refs, rows, kv_t[hd:2 * hd, :])
    kb_ref[...] = kv[:, 0:hd].astype(_BF16)
    vtile_ref[...] = kv_t[hd:2 * hd, :].astype(_BF16)
    z = _dot(xn, wf_ref[...]) + bf_ref[...]
    logf = jnp.minimum(z, 0.0) - jnp.log1p(jnp.exp(-jnp.abs(z)))
    lane = lax.broadcasted_iota(jnp.int32, logf.shape, 1)
    lf_ref[...] = jnp.where(lane < n_heads, logf, 0.0)


def _kv_project(x, streams, g, w_kv, w_fg_pad, b_fg_pad, n_heads):
    n, d = x.shape
    hd = w_kv.shape[1] // 2
    nb = n // TOKEN_TILE
    rows = tuple(s * t for s, t in streams)
    shapes = [(s, t) if t % TOKEN_TILE == 0 else (1, s * t) for s, t in streams]
    row = lambda width: pl.BlockSpec((TOKEN_TILE, width), lambda i: (i, 0))

    def transposed_specs():
        specs, start = [], 0
        for (_, cols), nblk in zip(shapes, _slab_blocks(rows)):
            def index(i, start=start, nblk=nblk, nt=cols // TOKEN_TILE):
                blk = jnp.clip(i - start, 0, nblk - 1)
                return (blk // nt, 0, blk % nt)
            mode = dict(pipeline_mode=pl.Buffered(1)) if nblk == 1 else {}
            specs.append(pl.BlockSpec((None, hd, TOKEN_TILE), index, **mode))
            start += nblk
        return specs

    transposed_shapes = [jax.ShapeDtypeStruct((s, hd, cols), _F32) for s, cols in shapes]
    out = pl.pallas_call(
        functools.partial(_kv_body, rows=rows, hd=hd, n_heads=n_heads),
        grid=(nb,),
        in_specs=[row(d), _const_spec((1, d)), _const_spec(w_kv.shape), _const_spec(w_fg_pad.shape),
                  _const_spec((1, LANES))],
        out_specs=transposed_specs() + transposed_specs() + [
            row(LANES), row(hd), pl.BlockSpec((None, hd, TOKEN_TILE), lambda i: (i, 0, 0))],
        out_shape=transposed_shapes + transposed_shapes + [
            jax.ShapeDtypeStruct((n, LANES), _F32), jax.ShapeDtypeStruct((n, hd), _BF16),
            jax.ShapeDtypeStruct((nb, hd, TOKEN_TILE), _BF16)],
        compiler_params=_params("arbitrary"),
        name="kv_project",
    )(x, g.reshape(1, d), w_kv, w_fg_pad, b_fg_pad)
    ns = len(rows)
    return out[:ns], out[ns:2 * ns], out[2 * ns], out[2 * ns + 1], out[2 * ns + 2]


def _cache_body(ckt_ref, cvt_ref, kn_ref, vn_ref, k_ref, vt_ref, *, n_cache_blocks, dec_seq, new_row0):
    b = pl.program_id(0)
    j = pl.program_id(1)

    @pl.when(j < n_cache_blocks)
    def _():
        k_ref[...] = ckt_ref[...].T.astype(_BF16)
        vt_ref[...] = cvt_ref[...].astype(_BF16)

    @pl.when(j >= n_cache_blocks)
    def _():
        hd = k_ref.shape[1]
        pad = jnp.zeros((ATT_TK - dec_seq, hd), _F32)
        k_ref[...] = jnp.concatenate([kn_ref[...].astype(_F32), pad], axis=0).astype(_BF16)
        off = (new_row0 + b * dec_seq) % TOKEN_TILE
        r = lax.broadcasted_iota(jnp.int32, (TOKEN_TILE, ATT_TK), 0)
        c = lax.broadcasted_iota(jnp.int32, (TOKEN_TILE, ATT_TK), 1)
        select = jnp.where((r == off + c) & (c < dec_seq), 1.0, 0.0).astype(_BF16)
        vt_ref[...] = _dot(vn_ref[...], select).astype(_BF16)


def _cache_keys(cache_kt, cache_vt, k_new_bf, vt_new, *, new_row0, dec_seq):
    s, hd, past = cache_kt.shape
    assert past % ATT_TK == 0 and dec_seq % 16 == 0 and TOKEN_TILE % dec_seq == 0 and new_row0 % dec_seq == 0
    ncb = past // ATT_TK
    nkb = ncb + 1
    cache = pl.BlockSpec((None, hd, ATT_TK), lambda b, j: (b, 0, jnp.minimum(j, ncb - 1)))
    return pl.pallas_call(
        functools.partial(_cache_body, n_cache_blocks=ncb, dec_seq=dec_seq, new_row0=new_row0),
        grid=(s, nkb),
        in_specs=[cache, cache,
                  pl.BlockSpec((dec_seq, hd), lambda b, j: (new_row0 // dec_seq + b, 0)),
                  pl.BlockSpec((None, hd, TOKEN_TILE), lambda b, j: ((new_row0 + b * dec_seq) // TOKEN_TILE, 0, 0))],
        out_specs=[pl.BlockSpec((ATT_TK, hd), lambda b, j: (b * nkb + j, 0)),
                   pl.BlockSpec((None, hd, ATT_TK), lambda b, j: (b * nkb + j, 0, 0))],
        out_shape=[jax.ShapeDtypeStruct((s * nkb * ATT_TK, hd), _BF16),
                   jax.ShapeDtypeStruct((s * nkb, hd, ATT_TK), _BF16)],
        compiler_params=_params("arbitrary", "arbitrary"),
        name="cache_keys",
    )(cache_kt, cache_vt, k_new_bf, vt_new)


def _decay_body(lf_ref, e_ref, c_ref, *, n_heads):
    n_chunks = lf_ref.shape[0] // CUMSUM_CHUNK
    r = lax.broadcasted_iota(jnp.int32, (CUMSUM_CHUNK, CUMSUM_CHUNK), 0)
    c = lax.broadcasted_iota(jnp.int32, (CUMSUM_CHUNK, CUMSUM_CHUNK), 1)
    tril = jnp.where(r >= c, 1.0, 0.0).astype(_BF16)
    sr = lax.broadcasted_iota(jnp.int32, (LANES, LANES), 0)
    sc = lax.broadcasted_iota(jnp.int32, (LANES, LANES), 1)
    place = [jnp.where((sc == 3 * sr + j) & (sr < n_heads), 1.0, 0.0).astype(_BF16) for j in range(3)]

    for i in range(n_chunks):
        rows = slice(i * CUMSUM_CHUNK, (i + 1) * CUMSUM_CHUNK)
        hi, mid, lo = _split3(lf_ref[rows, :])
        c_ref[rows, :] = _dot(tril, hi) + _dot(tril, mid) + _dot(tril, lo)
    carry = jnp.zeros((1, LANES), _F32)
    for i in range(n_chunks):
        rows = slice(i * CUMSUM_CHUNK, (i + 1) * CUMSUM_CHUNK)
        csum = c_ref[rows, :] + carry
        carry = carry + c_ref[(i + 1) * CUMSUM_CHUNK - 1:(i + 1) * CUMSUM_CHUNK, :]
        p0, p1, p2 = _split3(-LOG2_E * csum)
        feats = _dot(p0, place[0]) + _dot(p1, place[1]) + _dot(p2, place[2])
        e_ref[rows, :] = feats.astype(_BF16)


def _decay_features(logf, n_heads):
    b, t, _ = logf.shape
    assert t % CUMSUM_CHUNK == 0 and 3 * n_heads <= LANES
    blk = pl.BlockSpec((None, t, LANES), lambda i: (i, 0, 0))
    return pl.pallas_call(
        functools.partial(_decay_body, n_heads=n_heads),
        grid=(b,),
        in_specs=[blk],
        out_specs=blk,
        out_shape=jax.ShapeDtypeStruct((b, t, LANES), _BF16),
        scratch_shapes=[pltpu.VMEM((t, LANES), _F32)],
        compiler_params=_params("arbitrary"),
        name="decay_features",
    )(logf)


def _attn_group(n_sub, q_off):
    return ATT_TK // ATT_TQ if (n_sub * ATT_TQ) % ATT_TK == 0 and q_off % ATT_TK == 0 else 1


def _key_block_pipeline(n_full, scores, consume, state):
    def double_trip(t, carry):
        maxima, state = carry
        j = 2 * t
        prev = jnp.where(t == 0, n_full, j - 1)
        maxima_a = scores(j, 1, False)
        state = consume(prev, 0, maxima, state)
        maxima_b = scores(j + 1, 0, False)
        state = consume(j, 1, maxima_a, state)
        return maxima_b, state

    def odd_tail(carry):
        maxima, state = carry
        prev = jnp.where(n_full == 1, n_full, n_full - 2)
        maxima_a = scores(n_full - 1, 1, False)
        state = consume(prev, 0, maxima, state)
        return consume(n_full - 1, 1, maxima_a, state)

    def even_tail(carry):
        maxima, state = carry
        last = jnp.where(n_full == 0, n_full, n_full - 1)
        return consume(last, 0, maxima, state)

    carry = (scores(n_full, 0, True), state)
    carry = lax.fori_loop(0, n_full // 2, double_trip, carry)
    return lax.cond(n_full % 2 == 1, odd_tail, even_tail, carry)


def _attn_body(q_ref, k_ref, e_ref, vt_ref, o_ref, s_ref, *, n_sub, q_off, head_dim):
    pair = pl.program_id(1)
    qblk = pl.program_id(2)
    heads_per_block = LANES // head_dim
    group = _attn_group(n_sub, q_off)
    lane = lax.broadcasted_iota(jnp.int32, (ATT_TQ, LANES), 1)
    feat = lax.broadcasted_iota(jnp.int32, (LANES, ATT_TQ), 0)
    key_iota = lax.broadcasted_iota(jnp.int32, (ATT_TK, ATT_TQ), 0)
    qry_iota = lax.broadcasted_iota(jnp.int32, (ATT_TK, ATT_TQ), 1)
    chains = [(hh, s) for hh in range(heads_per_block) for s in range(group)]
    ones_rows = jnp.ones((SUM_ROWS, ATT_TK), _BF16)

    for g in range(n_sub // group):
        q0 = q_off + (qblk * n_sub + g * group) * ATT_TQ
        q_augs = []
        for hh, s in chains:
            head = pair * heads_per_block + hh
            bias_sel = jnp.where((feat >= 3 * head) & (feat < 3 * head + 3), 1.0, 0.0).astype(_BF16)
            head_lanes = (lane >= hh * head_dim) & (lane < (hh + 1) * head_dim)
            q = q_ref[(g * group + s) * ATT_TQ:(g * group + s + 1) * ATT_TQ, :].astype(_F32)
            q_t = jnp.where(head_lanes, q, 0.0).T.astype(_BF16)
            q_augs.append(jnp.concatenate([q_t, bias_sel], axis=0))

        def scores(j, slot, masked):
            rows = pl.ds(pl.multiple_of(j * ATT_TK, ATT_TK), ATT_TK)
            k_aug = jnp.concatenate([k_ref[rows, :], e_ref[rows, :]], axis=1)
            tiles = [_dot(k_aug, q_aug) for q_aug in q_augs]
            maxima = []
            for c, ((hh, s), sc) in enumerate(zip(chains, tiles)):
                if masked:
                    sc = jnp.where(key_iota + j * ATT_TK <= qry_iota + (q0 + s * ATT_TQ), sc, MASK_VALUE)
                s_ref[slot, c] = sc
                maxima.append(jnp.max(sc, axis=0, keepdims=True))
            return tuple(maxima)

        def consume(j, slot, maxima, state):
            out = []
            for c, ((hh, s), bm, (m, acc)) in enumerate(zip(chains, maxima, state)):
                m_new = jnp.maximum(m, bm)
                alpha = jnp.exp2(m - m_new)
                p = jnp.exp2(s_ref[slot, c] - m_new).astype(_BF16)
                vt = jnp.concatenate([vt_ref[j, hh * head_dim:(hh + 1) * head_dim, :], ones_rows], axis=0)
                out.append((m_new, alpha * acc + _dot(vt, p)))
            return tuple(out)

        init = (jnp.full((1, ATT_TQ), MASK_VALUE, _F32), jnp.zeros((head_dim + SUM_ROWS, ATT_TQ), _F32))
        final = _key_block_pipeline(q0 // ATT_TK, scores, consume, (init,) * len(chains))
        for (hh, s), (_, acc) in zip(chains, final):
            cols = slice((g * group + s) * ATT_TQ, (g * group + s + 1) * ATT_TQ)
            out = acc[0:head_dim, :] / acc[head_dim:head_dim + 1, :]
            o_ref[hh * head_dim:(hh + 1) * head_dim, cols] = out.astype(_BF16)


def _attn_packed_body(q_ref, k_ref, e_ref, vt_ref, o_ref, s_ref, *, pairs, q_off, head_dim):
    pair0 = pl.program_id(1) * pairs
    half = ATT_TQ // 2
    q0 = q_off
    feat = lax.broadcasted_iota(jnp.int32, (LANES, ATT_TQ), 0)
    col_head = lax.broadcasted_iota(jnp.int32, (LANES, ATT_TQ), 1) // half
    key_iota = lax.broadcasted_iota(jnp.int32, (ATT_TK, ATT_TQ), 0)
    qry_iota = lax.broadcasted_iota(jnp.int32, (ATT_TK, ATT_TQ), 1) % half
    ones_rows = jnp.ones((SUM_ROWS, ATT_TK), _BF16)

    q_augs = []
    for pp in range(pairs):
        q_t = q_ref[:, pp * LANES:(pp + 1) * LANES].astype(_F32).T
        q_t = jnp.concatenate([q_t, q_t], axis=1)
        q_t = jnp.where(feat // head_dim == col_head, q_t, 0.0).astype(_BF16)
        head = (pair0 + pp) * (LANES // head_dim) + col_head
        bias_sel = jnp.where((feat >= 3 * head) & (feat < 3 * head + 3), 1.0, 0.0).astype(_BF16)
        q_augs.append(jnp.concatenate([q_t, bias_sel], axis=0))

    def scores(j, slot, masked):
        rows = pl.ds(pl.multiple_of(j * ATT_TK, ATT_TK), ATT_TK)
        tiles = [_dot(jnp.concatenate([k_ref[rows, pp * LANES:(pp + 1) * LANES], e_ref[rows, :]], axis=1),
                      q_augs[pp]) for pp in range(pairs)]
        maxima = []
        for pp, sc in enumerate(tiles):
            if masked:
                sc = jnp.where(key_iota + j * ATT_TK <= qry_iota + q0, sc, MASK_VALUE)
            s_ref[slot, pp] = sc
            maxima.append(jnp.max(sc, axis=0, keepdims=True))
        return tuple(maxima)

    def consume(j, slot, maxima, state):
        out = []
        for pp, (bm, (m, acc)) in enumerate(zip(maxima, state)):
            m_new = jnp.maximum(m, bm)
            alpha = jnp.exp2(m - m_new)
            p = jnp.exp2(s_ref[slot, pp] - m_new).astype(_BF16)
            vt = jnp.concatenate([vt_ref[j, pp * LANES:(pp + 1) * LANES, :], ones_rows], axis=0)
            out.append((m_new, alpha * acc + _dot(vt, p)))
        return tuple(out)

    init = (jnp.full((1, ATT_TQ), MASK_VALUE, _F32), jnp.zeros((LANES + SUM_ROWS, ATT_TQ), _F32))
    final = _key_block_pipeline(jnp.int32(q0 // ATT_TK), scores, consume, (init,) * pairs)
    for pp, (_, acc) in enumerate(final):
        for hh in range(LANES // head_dim):
            cols = slice(hh * half, (hh + 1) * half)
            out = acc[hh * head_dim:(hh + 1) * head_dim, cols] / acc[LANES:LANES + 1, cols]
            o_ref[pp * LANES + hh * head_dim:pp * LANES + (hh + 1) * head_dim, :] = out.astype(_BF16)


def _attention(q, k, e, vt, *, n_streams, t_q, t_k, q_off, q_row0, k_row0, head_dim, packed_pairs=0):
    hd = q.shape[1]
    n_pairs = hd // LANES
    nkb = t_k // ATT_TK
    kb0, vb0 = k_row0 // t_k, k_row0 // ATT_TK
    assert t_k % ATT_TK == 0 and k_row0 % t_k == 0 and vb0 % nkb == 0 and q_off + t_q <= t_k
    if packed_pairs:
        width = packed_pairs * LANES
        assert t_q == ATT_TQ // 2 and q_off % ATT_TK == 0 and n_pairs % packed_pairs == 0 and q_row0 % t_q == 0
        body = functools.partial(_attn_packed_body, pairs=packed_pairs, q_off=q_off, head_dim=head_dim)
        grid, tq_blk, n_chains, nq = (n_streams, n_pairs // packed_pairs, 1), t_q, packed_pairs, 1
    else:
        width = LANES
        tq_blk = min(t_q, 4 * ATT_TQ)
        assert t_q % tq_blk == 0 and tq_blk % ATT_TQ == 0 and q_off % ATT_TQ == 0 and q_row0 % tq_blk == 0
        nq = t_q // tq_blk
        n_sub = tq_blk // ATT_TQ
        body = functools.partial(_attn_body, n_sub=n_sub, q_off=q_off, head_dim=head_dim)
        grid, n_chains = (n_streams, n_pairs, nq), (LANES // head_dim) * _attn_group(n_sub, q_off)
    qb0 = q_row0 // tq_blk
    return pl.pallas_call(
        body,
        grid=grid,
        in_specs=[
            pl.BlockSpec((tq_blk, width), lambda b, p, i: (qb0 + b * nq + i, p)),
            pl.BlockSpec((t_k, width), lambda b, p, i: (kb0 + b, p)),
            pl.BlockSpec((None, t_k, LANES), lambda b, p, i: (b, 0, 0)),
            pl.BlockSpec((nkb, width, ATT_TK), lambda b, p, i: (vb0 // nkb + b, p, 0)),
        ],
        out_specs=pl.BlockSpec((width, tq_blk), lambda b, p, i: (p, b * nq + i)),
        out_shape=jax.ShapeDtypeStruct((hd, n_streams * t_q), _BF16),
        scratch_shapes=[pltpu.VMEM((2, n_chains, ATT_TK, ATT_TQ), _F32)],
        compiler_params=_params("arbitrary", "arbitrary", "arbitrary"),
        name="forget_attention",
    )(q, k, e, vt)


def kernel(x_prompt, x_sample, cache_pool, cache_k, cache_v, cache_logf, ln_ffn1, ln_mix, ln_ffn2, w_ffn_in, w_ffn_out, w_pool, pool_scale, ln_kv, w_kv, w_fgate, b_fgate, w_q, w_o, ln_final):
    batch, seq, d = x_prompt.shape
    dec_batch, dec_seq, _ = x_sample.shape
    past_len, n_heads, head_dim = cache_k.shape[1:]
    hd = n_heads * head_dim
    depth = ln_ffn1.shape[0]
    n_a = w_pool.shape[0]
    n_prompt, n_sample = batch * seq, dec_batch * dec_seq
    rows = (n_prompt, n_sample)
    assert head_dim * 2 == LANES and seq % TOKEN_TILE == 0 and TOKEN_TILE == ATT_TK

    win, wout = w_ffn_in, w_ffn_out
    wpool = w_pool.astype(_BF16)
    wkv = w_kv.astype(_BF16)
    wq = w_q.astype(_BF16)
    wo = w_o.astype(_BF16)
    wfg = jnp.pad(w_fgate, ((0, 0), (0, LANES - n_heads))).astype(_BF16)
    bfg = jnp.pad(b_fgate, (0, LANES - n_heads)).reshape(1, LANES)

    def time_minor(cache):
        return cache.transpose(0, 2, 3, 1).reshape(cache.shape[0], hd, cache.shape[1])

    def time_major(x_t, n_streams, t_len):
        g = x_t.shape[0]
        x_t = x_t.reshape(g, n_heads, head_dim, n_streams // g, t_len)
        return x_t.transpose(0, 3, 4, 1, 2).reshape(n_streams, t_len, n_heads, head_dim)

    packed_pairs = 4 if dec_seq <= ATT_TQ // 2 and (hd // LANES) % 4 == 0 else 0
    sample_q = ATT_TQ // 2 if packed_pairs else -(-dec_seq // ATT_TQ) * ATT_TQ
    sample_keys = past_len + ATT_TK
    assert sample_q <= ATT_TK

    slabs = [x_prompt.reshape(n_prompt, d), x_sample.reshape(n_sample, d)]
    new_pool_prompt, new_pool_sample = [], []
    ks = vs = logf128 = k_bf = vt = e_prompt = e_sample = k_bf_sample = vt_sample = None
    for l in range(depth):
        last = l == depth - 1
        attn = None
        if l < n_a:
            x = _ffn(slabs, ln_ffn1[l], win, wout, (l, 0))
            hist_p = jnp.zeros((batch, POOL_HIST_ROWS, d), _F32)
            hist_s = jnp.pad(cache_pool[l], ((0, 0), (POOL_HIST_ROWS - POOL_HIST, 0), (0, 0)))
            xp, pool_p = _pool_mix(x, hist_p, ln_mix[l], wpool[l], pool_scale[l],
                                   n_streams=batch, t_len=seq, row0=0, pos0=0)
            xs, pool_s = _pool_mix(x, hist_s, ln_mix[l], wpool[l], pool_scale[l],
                                   n_streams=dec_batch, t_len=dec_seq, row0=n_prompt, pos0=past_len)
            slabs = [xp, xs]
            new_pool_prompt.append(pool_p[:, POOL_HIST_ROWS - POOL_HIST:])
            new_pool_sample.append(pool_s[:, POOL_HIST_ROWS - POOL_HIST:])
        else:
            j = l - n_a
            x, q_bf = _ffn(slabs, ln_ffn1[l], win, wout, (l, 0),
                           query=(ln_mix[l], wq[j], head_dim ** -0.5 * LOG2_E))
            o_prompt = _attention(q_bf, k_bf, e_prompt, vt, n_streams=batch, t_q=seq, t_k=seq, q_off=0,
                                  q_row0=0, k_row0=0, head_dim=head_dim)
            q_s = jnp.pad(q_bf[n_prompt:].reshape(dec_batch, dec_seq, hd),
                          ((0, 0), (0, sample_q - dec_seq), (0, 0))).reshape(dec_batch * sample_q, hd)
            o_sample = _attention(q_s, k_bf_sample, e_sample, vt_sample, n_streams=dec_batch, t_q=sample_q,
                                  t_k=sample_keys, q_off=past_len, q_row0=0, k_row0=0, head_dim=head_dim,
                                  packed_pairs=packed_pairs)
            o_sample = o_sample.reshape(hd, dec_batch, sample_q)[:, :, :dec_seq].reshape(hd, n_sample)
            slabs, attn = [x], ([o_prompt, o_sample], wo[j])
        x = _ffn(slabs, ln_ffn2[l], win, wout, (l, 1), attn=attn, g_final=ln_final if last else None,
                 out_rows=rows if last else None)
        slabs = x if last else [x]
        if l == n_a - 1:
            ks, vs, logf128, k_bf, vt = _kv_project(x, [(batch, seq), (dec_batch, dec_seq)], ln_kv, wkv,
                                                    wfg, bfg, n_heads)
            e_prompt = _decay_features(logf128[:n_prompt].reshape(batch, seq, LANES), n_heads)
            logf_s = jnp.concatenate([
                jnp.pad(cache_logf, ((0, 0), (0, 0), (0, LANES - n_heads))),
                logf128[n_prompt:].reshape(dec_batch, dec_seq, LANES),
                jnp.zeros((dec_batch, ATT_TK - dec_seq, LANES), _F32)], axis=1)
            e_sample = _decay_features(logf_s, n_heads)
            k_bf_sample, vt_sample = _cache_keys(
                time_minor(cache_k), time_minor(cache_v), k_bf, vt, new_row0=n_prompt, dec_seq=dec_seq)

    y_prompt, y_sample = slabs
    logf = logf128[:, :n_heads]
    return (y_prompt.reshape(batch, seq, d), y_sample.reshape(dec_batch, dec_seq, d),
            jnp.stack(new_pool_prompt), jnp.stack(new_pool_sample),
            time_major(ks[0], batch, seq), time_major(vs[0], batch, seq),
            logf[:n_prompt].reshape(batch, seq, n_heads),
            time_major(ks[1], dec_batch, dec_seq), time_major(vs[1], dec_batch, dec_seq),
            logf[n_prompt:].reshape(dec_batch, dec_seq, n_heads))
```

```python
import functools

import jax
import jax.numpy as jnp
from jax import lax
from jax.experimental import pallas as pl
from jax.experimental.pallas import tpu as pltpu

RMS_EPS = 1e-6
POOL_WINDOWS = (2, 4, 8, 16)
POOL_HIST = max(POOL_WINDOWS) - 1

LANES = 128
TOKEN_TILE = 512
FFN_CHUNK = 256
POOL_PAD = 8
POOL_HIST_ROWS = 16
CUMSUM_CHUNK = 256
ATT_TQ = 256
ATT_TK = 512
SUM_ROWS = 16
MASK_VALUE = -1e30
LOG2_E = 1.4426950408889634
VMEM_LIMIT = 52 * 1024 * 1024

_BF16 = jnp.bfloat16
_F32 = jnp.float32


def _rms(x, g):
    ms = jnp.mean(x * x, axis=-1, keepdims=True)
    return x * lax.rsqrt(ms + RMS_EPS) * g


def _dot(a, b):
    return jnp.dot(a, b, preferred_element_type=_F32)


def _split3(x):
    hi = x.astype(_BF16)
    r = x - hi.astype(_F32)
    mid = r.astype(_BF16)
    lo = (r - mid.astype(_F32)).astype(_BF16)
    return hi, mid, lo


def _params(*sem):
    return pltpu.CompilerParams(dimension_semantics=sem, vmem_limit_bytes=VMEM_LIMIT)


def _const_spec(shape):
    zeros = (0,) * len(shape)
    return pl.BlockSpec(shape, lambda *_: zeros, pipeline_mode=pl.Buffered(1))


def _slab_blocks(rows):
    assert all(n % TOKEN_TILE == 0 for n in rows)
    return [n // TOKEN_TILE for n in rows]


def _slab_specs(rows, width, transposed=False):
    tail = width if isinstance(width, tuple) else (width,)
    specs, start = [], 0
    for nb in _slab_blocks(rows):
        def index(i, start=start, nb=nb):
            blk = jnp.clip(i - start, 0, nb - 1)
            return (0, blk) if transposed else (blk,) + (0,) * len(tail)
        shape = tail + (TOKEN_TILE,) if transposed else (TOKEN_TILE,) + tail
        mode = dict(pipeline_mode=pl.Buffered(1)) if nb == 1 else {}
        specs.append(pl.BlockSpec(shape, index, **mode))
        start += nb
    return specs


def _slab_read(refs, rows):
    i = pl.program_id(0)
    blocks = _slab_blocks(rows)
    value, end = refs[-1][...], sum(blocks)
    for ref, nb in zip(refs[-2::-1], blocks[:0:-1]):
        end -= nb
        value = jnp.where(i < end, ref[...], value)
    return value


def _slab_write(refs, rows, value):
    i = pl.program_id(0)
    start = 0
    for ref, nb in zip(refs, _slab_blocks(rows)):
        @pl.when((i >= start) & (i < start + nb))
        def _(ref=ref):
            ref[...] = value
        start += nb


def _slab_shapes(rows, width, dtype):
    tail = width if isinstance(width, tuple) else (width,)
    return [jax.ShapeDtypeStruct((n,) + tail, dtype) for n in rows]


def _ffn_body(*refs, rows_in, rows_attn, rows_out, d_ff, final_norm, query_scale):
    refs = list(refs)
    take = lambda n: [refs.pop(0) for _ in range(n)]
    x_refs = take(len(rows_in))
    if rows_attn:
        ot_refs, (wo_ref,) = take(len(rows_attn)), take(1)
    g_ref, win_ref, wout_ref = take(3)
    if final_norm:
        (gf_ref,) = take(1)
    if query_scale is not None:
        gq_ref, wq_ref = take(2)
    o_refs = take(len(rows_out))
    if query_scale is not None:
        (q_ref,) = take(1)
    (acc_ref,) = refs

    x = _slab_read(x_refs, rows_in)
    if rows_attn:
        o_t = _slab_read(ot_refs, rows_attn)
        x = x + lax.dot_general(o_t, wo_ref[...], (((0,), (0,)), ((), ())), preferred_element_type=_F32)
    xn = _rms(x, g_ref[...]).astype(_BF16)
    for c in range(d_ff // FFN_CHUNK):
        lo, hi = c * FFN_CHUNK, (c + 1) * FFN_CHUNK
        g = _dot(xn, win_ref[:, lo:hi].astype(_BF16))
        u = _dot(xn, win_ref[:, d_ff + lo:d_ff + hi].astype(_BF16))
        a = (g * (1.0 / (1.0 + jnp.exp(-g))) * u).astype(_BF16)
        part = _dot(a, wout_ref[lo:hi, :].astype(_BF16))
        if c == 0:
            acc_ref[...] = part
        else:
            acc_ref[...] += part
    y = x + 0.5 * acc_ref[...]
    if query_scale is not None:
        u_q = _rms(y, gq_ref[...]).astype(_BF16)
        q_ref[...] = (_dot(u_q, wq_ref[...]) * query_scale).astype(_BF16)
    if final_norm:
        y = _rms(y, gf_ref[...])
    _slab_write(o_refs, rows_out, y)


def _ffn(xs, g, w_in_all, w_out_all, which, *, attn=None, query=None, g_final=None, out_rows=None):
    d = xs[0].shape[1]
    d_ff = w_out_all.shape[2]
    assert d_ff % FFN_CHUNK == 0
    rows_in = tuple(x.shape[0] for x in xs)
    rows_out = tuple(out_rows) if out_rows else (sum(rows_in),)
    assert sum(rows_out) == sum(rows_in)
    n = sum(rows_in)
    picked = lambda w: pl.BlockSpec((None, None) + w.shape[2:], lambda i: tuple(which) + (0, 0),
                                    pipeline_mode=pl.Buffered(1))
    in_specs, args, rows_attn = _slab_specs(rows_in, d), list(xs), ()
    if attn is not None:
        o_ts, w_o = attn
        rows_attn = tuple(o.shape[1] for o in o_ts)
        assert sum(rows_attn) == n
        in_specs += _slab_specs(rows_attn, w_o.shape[0], transposed=True) + [_const_spec(w_o.shape)]
        args += list(o_ts) + [w_o]
    in_specs += [_const_spec((1, d)), picked(w_in_all), picked(w_out_all)]
    args += [g.reshape(1, d), w_in_all, w_out_all]
    if g_final is not None:
        in_specs.append(_const_spec((1, d)))
        args.append(g_final.reshape(1, d))
    out_specs, out_shape = _slab_specs(rows_out, d), _slab_shapes(rows_out, d, _F32)
    if query is not None:
        g_q, w_q, scale = query
        in_specs += [_const_spec((1, d)), _const_spec(w_q.shape)]
        args += [g_q.reshape(1, d), w_q]
        out_specs = out_specs + [pl.BlockSpec((TOKEN_TILE, w_q.shape[1]), lambda i: (i, 0))]
        out_shape = out_shape + [jax.ShapeDtypeStruct((n, w_q.shape[1]), _BF16)]
    out = pl.pallas_call(
        functools.partial(_ffn_body, rows_in=rows_in, rows_attn=rows_attn, rows_out=rows_out, d_ff=d_ff,
                          final_norm=g_final is not None, query_scale=None if query is None else query[2]),
        grid=(n // TOKEN_TILE,),
        in_specs=in_specs,
        out_specs=out_specs,
        out_shape=out_shape,
        scratch_shapes=[pltpu.VMEM((TOKEN_TILE, d), _F32)],
        compiler_params=_params("arbitrary"),
        name="ffn",
    )(*args)
    y = out[:len(rows_out)] if out_rows else out[0]
    return (y, out[-1]) if query is not None else y


def _pool_body(x_ref, hist_ref, g_ref, wp_ref, sc_ref, o_ref, pool_ref, buf, s2, s4, s8, *, tt, pos0):
    t = pl.program_id(1)
    d = x_ref.shape[-1]
    gw = d // len(POOL_WINDOWS)
    off = POOL_PAD + POOL_HIST_ROWS
    end = off + tt

    @pl.when(t == 0)
    def _():
        zeros = jnp.zeros((POOL_PAD, d), _F32)
        buf[0:POOL_PAD, :] = zeros
        s2[0:POOL_PAD, :] = zeros
        s4[0:POOL_PAD, :] = zeros
        buf[POOL_PAD:off, :] = hist_ref[...]

    @pl.when(t > 0)
    def _():
        buf[POOL_PAD:off, :] = buf[end - POOL_HIST_ROWS:end, :]

    x = x_ref[...]
    u = _rms(x, g_ref[...])
    buf[off:end, :] = u

    s2[POOL_PAD:end, :] = buf[POOL_PAD:end, :] + buf[POOL_PAD - 1:end - 1, :]
    s4[POOL_PAD:end, gw:] = s2[POOL_PAD:end, gw:] + s2[POOL_PAD - 2:end - 2, gw:]
    s8[POOL_PAD:end, 2 * gw:] = s4[POOL_PAD:end, 2 * gw:] + s4[POOL_PAD - 4:end - 4, 2 * gw:]
    sums = (
        s2[off:end, 0:gw],
        s4[off:end, gw:2 * gw],
        s8[off:end, 2 * gw:3 * gw],
        s8[off:end, 3 * gw:] + s8[off - 8:end - 8, 3 * gw:],
    )
    pos = pos0 + t * tt + lax.broadcasted_iota(jnp.int32, (tt, 1), 0)
    for gi, w in enumerate(POOL_WINDOWS):
        sl = slice(gi * gw, (gi + 1) * gw)
        inv_cnt = 1.0 / jnp.minimum(pos + 1, w).astype(_F32)
        diff = (sums[gi] * inv_cnt - u[:, sl]).astype(_BF16)
        y = _dot(diff, wp_ref[gi])
        o_ref[:, sl] = x[:, sl] + y * sc_ref[:, sl]

    @pl.when(t == pl.num_programs(1) - 1)
    def _():
        pool_ref[...] = buf[end - POOL_HIST_ROWS:end, :]


def _pool_mix(x_all, hist16, g, w_pool, scale, *, n_streams, t_len, row0, pos0):
    d = x_all.shape[1]
    tt = min(TOKEN_TILE, t_len)
    assert t_len % tt == 0 and row0 % tt == 0
    nt = t_len // tt
    blk0 = row0 // tt
    hist = pl.BlockSpec((None, POOL_HIST_ROWS, d), lambda b, t: (b, 0, 0))
    rows_buf = POOL_PAD + POOL_HIST_ROWS + tt
    return pl.pallas_call(
        functools.partial(_pool_body, tt=tt, pos0=pos0),
        grid=(n_streams, nt),
        in_specs=[pl.BlockSpec((tt, d), lambda b, t: (blk0 + b * nt + t, 0)), hist, _const_spec((1, d)),
                  _const_spec(w_pool.shape), _const_spec((1, d))],
        out_specs=[pl.BlockSpec((tt, d), lambda b, t: (b * nt + t, 0)), hist],
        out_shape=[jax.ShapeDtypeStruct((n_streams * t_len, d), _F32),
                   jax.ShapeDtypeStruct((n_streams, POOL_HIST_ROWS, d), _F32)],
        scratch_shapes=[pltpu.VMEM((rows_buf, d), _F32)] * 4,
        compiler_params=_params("arbitrary", "arbitrary"),
        name="pool_mix",
    )(x_all, hist16, g.reshape(1, d), w_pool, scale.reshape(1, d))


def _kv_body(x_ref, g_ref, wkv_ref, wf_ref, bf_ref, *out_refs, rows, hd, n_heads):
    ns = len(rows)
    kt_refs, vt_refs = out_refs[:ns], out_refs[ns:2 * ns]
    lf_ref, kb_ref, vtile_ref = out_refs[2 * ns:]
    xn = _rms(x_ref[...], g_ref[...]).astype(_BF16)
    kv = _dot(xn, wkv_ref[...])
    kv_t = kv.T
    _slab_write(kt_refs, rows, kv_t[0:hd, :])
    _slab_write(vt_refs, rows, kv_t[hd:2 * hd, :])
    kb_ref[...] = kv[:, 0:hd].astype(_BF16)
    vtile_ref[...] = kv_t[hd:2 * hd, :].astype(_BF16)
    z = _dot(xn, wf_ref[...]) + bf_ref[...]
    logf = jnp.minimum(z, 0.0) - jnp.log1p(jnp.exp(-jnp.abs(z)))
    lane = lax.broadcasted_iota(jnp.int32, logf.shape, 1)
    lf_ref[...] = jnp.where(lane < n_heads, logf, 0.0)


def _kv_project(x, streams, g, w_kv, w_fg_pad, b_fg_pad, n_heads):
    n, d = x.shape
    hd = w_kv.shape[1] // 2
    nb = n // TOKEN_TILE
    rows = tuple(s * t for s, t in streams)
    shapes = [(s, t) if t % TOKEN_TILE == 0 else (1, s * t) for s, t in streams]
    row = lambda width: pl.BlockSpec((TOKEN_TILE, width), lambda i: (i, 0))

    def transposed_specs():
        specs, start = [], 0
        for (_, cols), nblk in zip(shapes, _slab_blocks(rows)):
            def index(i, start=start, nblk=nblk, nt=cols // TOKEN_TILE):
                blk = jnp.clip(i - start, 0, nblk - 1)
                return (blk // nt, 0, blk % nt)
            mode = dict(pipeline_mode=pl.Buffered(1)) if nblk == 1 else {}
            specs.append(pl.BlockSpec((None, hd, TOKEN_TILE), index, **mode))
            start += nblk
        return specs

    transposed_shapes = [jax.ShapeDtypeStruct((s, hd, cols), _F32) for s, cols in shapes]
    out = pl.pallas_call(
        functools.partial(_kv_body, rows=rows, hd=hd, n_heads=n_heads),
        grid=(nb,),
        in_specs=[row(d), _const_spec((1, d)), _const_spec(w_kv.shape), _const_spec(w_fg_pad.shape),
                  _const_spec((1, LANES))],
        out_specs=transposed_specs() + transposed_specs() + [
            row(LANES), row(hd), pl.BlockSpec((None, hd, TOKEN_TILE), lambda i: (i, 0, 0))],
        out_shape=transposed_shapes + transposed_shapes + [
            jax.ShapeDtypeStruct((n, LANES), _F32), jax.ShapeDtypeStruct((n, hd), _BF16),
            jax.ShapeDtypeStruct((nb, hd, TOKEN_TILE), _BF16)],
        compiler_params=_params("arbitrary"),
        name="kv_project",
    )(x, g.reshape(1, d), w_kv, w_fg_pad, b_fg_pad)
    ns = len(rows)
    return out[:ns], out[ns:2 * ns], out[2 * ns], out[2 * ns + 1], out[2 * ns + 2]


def _cache_body(ckt_ref, cvt_ref, kn_ref, vn_ref, k_ref, vt_ref, *, n_cache_blocks, dec_seq, new_row0):
    b = pl.program_id(0)
    j = pl.program_id(1)

    @pl.when(j < n_cache_blocks)
    def _():
        k_ref[...] = ckt_ref[...].T.astype(_BF16)
        vt_ref[...] = cvt_ref[...].astype(_BF16)

    @pl.when(j >= n_cache_blocks)
    def _():
        hd = k_ref.shape[1]
        pad = jnp.zeros((ATT_TK - dec_seq, hd), _F32)
        k_ref[...] = jnp.concatenate([kn_ref[...].astype(_F32), pad], axis=0).astype(_BF16)
        off = (new_row0 + b * dec_seq) % TOKEN_TILE
        r = lax.broadcasted_iota(jnp.int32, (TOKEN_TILE, ATT_TK), 0)
        c = lax.broadcasted_iota(jnp.int32, (TOKEN_TILE, ATT_TK), 1)
        select = jnp.where((r == off + c) & (c < dec_seq), 1.0, 0.0).astype(_BF16)
        vt_ref[...] = _dot(vn_ref[...], select).astype(_BF16)


def _cache_keys(cache_kt, cache_vt, k_new_bf, vt_new, *, new_row0, dec_seq):
    s, hd, past = cache_kt.shape
    assert past % ATT_TK == 0 and dec_seq % 16 == 0 and TOKEN_TILE % dec_seq == 0 and new_row0 % dec_seq == 0
    ncb = past // ATT_TK
    nkb = ncb + 1
    cache = pl.BlockSpec((None, hd, ATT_TK), lambda b, j: (b, 0, jnp.minimum(j, ncb - 1)))
    return pl.pallas_call(
        functools.partial(_cache_body, n_cache_blocks=ncb, dec_seq=dec_seq, new_row0=new_row0),
        grid=(s, nkb),
        in_specs=[cache, cache,
                  pl.BlockSpec((dec_seq, hd), lambda b, j: (new_row0 // dec_seq + b, 0)),
                  pl.BlockSpec((None, hd, TOKEN_TILE), lambda b, j: ((new_row0 + b * dec_seq) // TOKEN_TILE, 0, 0))],
        out_specs=[pl.BlockSpec((ATT_TK, hd), lambda b, j: (b * nkb + j, 0)),
                   pl.BlockSpec((None, hd, ATT_TK), lambda b, j: (b * nkb + j, 0, 0))],
        out_shape=[jax.ShapeDtypeStruct((s * nkb * ATT_TK, hd), _BF16),
                   jax.ShapeDtypeStruct((s * nkb, hd, ATT_TK), _BF16)],
        compiler_params=_params("arbitrary", "arbitrary"),
        name="cache_keys",
    )(cache_kt, cache_vt, k_new_bf, vt_new)


def _decay_body(lf_ref, e_ref, c_ref, *, n_heads):
    n_chunks = lf_ref.shape[0] // CUMSUM_CHUNK
    r = lax.broadcasted_iota(jnp.int32, (CUMSUM_CHUNK, CUMSUM_CHUNK), 0)
    c = lax.broadcasted_iota(jnp.int32, (CUMSUM_CHUNK, CUMSUM_CHUNK), 1)
    tril = jnp.where(r >= c, 1.0, 0.0).astype(_BF16)
    sr = lax.broadcasted_iota(jnp.int32, (LANES, LANES), 0)
    sc = lax.broadcasted_iota(jnp.int32, (LANES, LANES), 1)
    place = [jnp.where((sc == 3 * sr + j) & (sr < n_heads), 1.0, 0.0).astype(_BF16) for j in range(3)]

    for i in range(n_chunks):
        rows = slice(i * CUMSUM_CHUNK, (i + 1) * CUMSUM_CHUNK)
        hi, mid, lo = _split3(lf_ref[rows, :])
        c_ref[rows, :] = _dot(tril, hi) + _dot(tril, mid) + _dot(tril, lo)
    carry = jnp.zeros((1, LANES), _F32)
    for i in range(n_chunks):
        rows = slice(i * CUMSUM_CHUNK, (i + 1) * CUMSUM_CHUNK)
        csum = c_ref[rows, :] + carry
        carry = carry + c_ref[(i + 1) * CUMSUM_CHUNK - 1:(i + 1) * CUMSUM_CHUNK, :]
        p0, p1, p2 = _split3(-LOG2_E * csum)
        feats = _dot(p0, place[0]) + _dot(p1, place[1]) + _dot(p2, place[2])
        e_ref[rows, :] = feats.astype(_BF16)


def _decay_features(logf, n_heads):
    b, t, _ = logf.shape
    assert t % CUMSUM_CHUNK == 0 and 3 * n_heads <= LANES
    blk = pl.BlockSpec((None, t, LANES), lambda i: (i, 0, 0))
    return pl.pallas_call(
        functools.partial(_decay_body, n_heads=n_heads),
        grid=(b,),
        in_specs=[blk],
        out_specs=blk,
        out_shape=jax.ShapeDtypeStruct((b, t, LANES), _BF16),
        scratch_shapes=[pltpu.VMEM((t, LANES), _F32)],
        compiler_params=_params("arbitrary"),
        name="decay_features",
    )(logf)


def _attn_group(n_sub, q_off):
    return ATT_TK // ATT_TQ if (n_sub * ATT_TQ) % ATT_TK == 0 and q_off % ATT_TK == 0 else 1


def _key_block_pipeline(n_full, scores, consume, state):
    def double_step(j, carry):
        maxima, state = carry
        prev = jnp.where(j == 0, n_full, j - 1)
        maxima_a = scores(j, 1, False)
        state = consume(prev, 0, maxima, state)
        maxima_b = scores(j + 1, 0, False)
        state = consume(j, 1, maxima_a, state)
        return maxima_b, state

    def odd_tail(carry):
        maxima, state = carry
        prev = jnp.where(n_full == 1, n_full, n_full - 2)
        maxima_a = scores(n_full - 1, 1, False)
        state = consume(prev, 0, maxima, state)
        return consume(n_full - 1, 1, maxima_a, state)

    def even_tail(carry):
        maxima, state = carry
        last = jnp.where(n_full == 0, n_full, n_full - 1)
        return consume(last, 0, maxima, state)

    carry = (scores(n_full, 0, True), state)
    carry = lax.fori_loop(0, n_full // 4, lambda t, c: double_step(4 * t + 2, double_step(4 * t, c)), carry)
    rest = n_full % 4
    carry = lax.cond(rest >= 2, lambda c: double_step(n_full - rest, c), lambda c: c, carry)
    return lax.cond(n_full % 2 == 1, odd_tail, even_tail, carry)


def _attn_body(q_ref, k_ref, e_ref, vt_ref, o_ref, s_ref, *, n_sub, q_off, head_dim):
    pair = pl.program_id(1)
    qblk = pl.program_id(2)
    heads_per_block = LANES // head_dim
    group = _attn_group(n_sub, q_off)
    lane = lax.broadcasted_iota(jnp.int32, (ATT_TQ, LANES), 1)
    feat = lax.broadcasted_iota(jnp.int32, (LANES, ATT_TQ), 0)
    key_iota = lax.broadcasted_iota(jnp.int32, (ATT_TK, ATT_TQ), 0)
    qry_iota = lax.broadcasted_iota(jnp.int32, (ATT_TK, ATT_TQ), 1)
    chains = [(hh, s) for hh in range(heads_per_block) for s in range(group)]
    ones_rows = jnp.ones((SUM_ROWS, ATT_TK), _BF16)

    for g in range(n_sub // group):
        q0 = q_off + (qblk * n_sub + g * group) * ATT_TQ
        q_augs = []
        for hh, s in chains:
            head = pair * heads_per_block + hh
            bias_sel = jnp.where((feat >= 3 * head) & (feat < 3 * head + 3), 1.0, 0.0).astype(_BF16)
            head_lanes = (lane >= hh * head_dim) & (lane < (hh + 1) * head_dim)
            q = q_ref[(g * group + s) * ATT_TQ:(g * group + s + 1) * ATT_TQ, :].astype(_F32)
            q_t = jnp.where(head_lanes, q, 0.0).T.astype(_BF16)
            q_augs.append(jnp.concatenate([q_t, bias_sel], axis=0))

        def scores(j, slot, masked):
            rows = pl.ds(pl.multiple_of(j * ATT_TK, ATT_TK), ATT_TK)
            k_aug = jnp.concatenate([k_ref[rows, :], e_ref[rows, :]], axis=1)
            tiles = [_dot(k_aug, q_aug) for q_aug in q_augs]
            maxima = []
            for c, ((hh, s), sc) in enumerate(zip(chains, tiles)):
                if masked:
                    sc = jnp.where(key_iota + j * ATT_TK <= qry_iota + (q0 + s * ATT_TQ), sc, MASK_VALUE)
                s_ref[slot, c] = sc
                maxima.append(jnp.max(sc, axis=0, keepdims=True))
            return tuple(maxima)

        def consume(j, slot, maxima, state):
            out = []
            for c, ((hh, s), bm, (m, acc)) in enumerate(zip(chains, maxima, state)):
                m_new = jnp.maximum(m, bm)
                alpha = jnp.exp2(m - m_new)
                p = jnp.exp2(s_ref[slot, c] - m_new).astype(_BF16)
                vt = jnp.concatenate([vt_ref[j, hh * head_dim:(hh + 1) * head_dim, :], ones_rows], axis=0)
                out.append((m_new, alpha * acc + _dot(vt, p)))
            return tuple(out)

        init = (jnp.full((1, ATT_TQ), MASK_VALUE, _F32), jnp.zeros((head_dim + SUM_ROWS, ATT_TQ), _F32))
        final = _key_block_pipeline(q0 // ATT_TK, scores, consume, (init,) * len(chains))
        for (hh, s), (_, acc) in zip(chains, final):
            cols = slice((g * group + s) * ATT_TQ, (g * group + s + 1) * ATT_TQ)
            out = acc[0:head_dim, :] / acc[head_dim:head_dim + 1, :]
            o_ref[hh * head_dim:(hh + 1) * head_dim, cols] = out.astype(_BF16)


def _attn_packed_body(q_ref, k_ref, e_ref, vt_ref, o_ref, s_ref, *, pairs, q_off, head_dim):
    pair0 = pl.program_id(1) * pairs
    half = ATT_TQ // 2
    q0 = q_off
    feat = lax.broadcasted_iota(jnp.int32, (LANES, ATT_TQ), 0)
    col_head = lax.broadcasted_iota(jnp.int32, (LANES, ATT_TQ), 1) // half
    key_iota = lax.broadcasted_iota(jnp.int32, (ATT_TK, ATT_TQ), 0)
    qry_iota = lax.broadcasted_iota(jnp.int32, (ATT_TK, ATT_TQ), 1) % half
    ones_rows = jnp.ones((SUM_ROWS, ATT_TK), _BF16)

    q_augs = []
    for pp in range(pairs):
        q_t = q_ref[:, pp * LANES:(pp + 1) * LANES].astype(_F32).T
        q_t = jnp.concatenate([q_t, q_t], axis=1)
        q_t = jnp.where(feat // head_dim == col_head, q_t, 0.0).astype(_BF16)
        head = (pair0 + pp) * (LANES // head_dim) + col_head
        bias_sel = jnp.where((feat >= 3 * head) & (feat < 3 * head + 3), 1.0, 0.0).astype(_BF16)
        q_augs.append(jnp.concatenate([q_t, bias_sel], axis=0))

    def scores(j, slot, masked):
        rows = pl.ds(pl.multiple_of(j * ATT_TK, ATT_TK), ATT_TK)
        tiles = [_dot(jnp.concatenate([k_ref[rows, pp * LANES:(pp + 1) * LANES], e_ref[rows, :]], axis=1),
                      q_augs[pp]) for pp in range(pairs)]
        maxima = []
        for pp, sc in enumerate(tiles):
            if masked:
                sc = jnp.where(key_iota + j * ATT_TK <= qry_iota + q0, sc, MASK_VALUE)
            s_ref[slot, pp] = sc
            maxima.append(jnp.max(sc, axis=0, keepdims=True))
        return tuple(maxima)

    def consume(j, slot, maxima, state):
        out = []
        for pp, (bm, (m, acc)) in enumerate(zip(maxima, state)):
            m_new = jnp.maximum(m, bm)
            alpha = jnp.exp2(m - m_new)
            p = jnp.exp2(s_ref[slot, pp] - m_new).astype(_BF16)
            vt = jnp.concatenate([vt_ref[j, pp * LANES:(pp + 1) * LANES, :], ones_rows], axis=0)
            out.append((m_new, alpha * acc + _dot(vt, p)))
        return tuple(out)

    init = (jnp.full((1, ATT_TQ), MASK_VALUE, _F32), jnp.zeros((LANES + SUM_ROWS, ATT_TQ), _F32))
    final = _key_block_pipeline(jnp.int32(q0 // ATT_TK), scores, consume, (init,) * pairs)
    for pp, (_, acc) in enumerate(final):
        for hh in range(LANES // head_dim):
            cols = slice(hh * half, (hh + 1) * half)
            out = acc[hh * head_dim:(hh + 1) * head_dim, cols] / acc[LANES:LANES + 1, cols]
            o_ref[pp * LANES + hh * head_dim:pp * LANES + (hh + 1) * head_dim, :] = out.astype(_BF16)


def _attention(q, k, e, vt, *, n_streams, t_q, t_k, q_off, q_row0, k_row0, head_dim, packed_pairs=0):
    hd = q.shape[1]
    n_pairs = hd // LANES
    nkb = t_k // ATT_TK
    kb0, vb0 = k_row0 // t_k, k_row0 // ATT_TK
    assert t_k % ATT_TK == 0 and k_row0 % t_k == 0 and vb0 % nkb == 0 and q_off + t_q <= t_k
    if packed_pairs:
        width = packed_pairs * LANES
        assert t_q == ATT_TQ // 2 and q_off % ATT_TK == 0 and n_pairs % packed_pairs == 0 and q_row0 % t_q == 0
        body = functools.partial(_attn_packed_body, pairs=packed_pairs, q_off=q_off, head_dim=head_dim)
        grid, tq_blk, n_chains, nq = (n_streams, n_pairs // packed_pairs, 1), t_q, packed_pairs, 1
    else:
        width = LANES
        tq_blk = min(t_q, 4 * ATT_TQ)
        assert t_q % tq_blk == 0 and tq_blk % ATT_TQ == 0 and q_off % ATT_TQ == 0 and q_row0 % tq_blk == 0
        nq = t_q // tq_blk
        n_sub = tq_blk // ATT_TQ
        body = functools.partial(_attn_body, n_sub=n_sub, q_off=q_off, head_dim=head_dim)
        grid, n_chains = (n_streams, n_pairs, nq), (LANES // head_dim) * _attn_group(n_sub, q_off)
    qb0 = q_row0 // tq_blk
    return pl.pallas_call(
        body,
        grid=grid,
        in_specs=[
            pl.BlockSpec((tq_blk, width), lambda b, p, i: (qb0 + b * nq + i, p)),
            pl.BlockSpec((t_k, width), lambda b, p, i: (kb0 + b, p)),
            pl.BlockSpec((None, t_k, LANES), lambda b, p, i: (b, 0, 0)),
            pl.BlockSpec((nkb, width, ATT_TK), lambda b, p, i: (vb0 // nkb + b, p, 0)),
        ],
        out_specs=pl.BlockSpec((width, tq_blk), lambda b, p, i: (p, b * nq + i)),
        out_shape=jax.ShapeDtypeStruct((hd, n_streams * t_q), _BF16),
        scratch_shapes=[pltpu.VMEM((2, n_chains, ATT_TK, ATT_TQ), _F32)],
        compiler_params=_params("arbitrary", "arbitrary", "arbitrary"),
        name="forget_attention",
    )(q, k, e, vt)


def kernel(x_prompt, x_sample, cache_pool, cache_k, cache_v, cache_logf, ln_ffn1, ln_mix, ln_ffn2, w_ffn_in, w_ffn_out, w_pool, pool_scale, ln_kv, w_kv, w_fgate, b_fgate, w_q, w_o, ln_final):
    batch, seq, d = x_prompt.shape
    dec_batch, dec_seq, _ = x_sample.shape
    past_len, n_heads, head_dim = cache_k.shape[1:]
    hd = n_heads * head_dim
    depth = ln_ffn1.shape[0]
    n_a = w_pool.shape[0]
    n_prompt, n_sample = batch * seq, dec_batch * dec_seq
    rows = (n_prompt, n_sample)
    assert head_dim * 2 == LANES and seq % TOKEN_TILE == 0 and TOKEN_TILE == ATT_TK

    win, wout = w_ffn_in, w_ffn_out
    wpool = w_pool.astype(_BF16)
    wkv = w_kv.astype(_BF16)
    wq = w_q.astype(_BF16)
    wo = w_o.astype(_BF16)
    wfg = jnp.pad(w_fgate, ((0, 0), (0, LANES - n_heads))).astype(_BF16)
    bfg = jnp.pad(b_fgate, (0, LANES - n_heads)).reshape(1, LANES)

    def time_minor(cache):
        return cache.transpose(0, 2, 3, 1).reshape(cache.shape[0], hd, cache.shape[1])

    def time_major(x_t, n_streams, t_len):
        g = x_t.shape[0]
        x_t = x_t.reshape(g, n_heads, head_dim, n_streams // g, t_len)
        return x_t.transpose(0, 3, 4, 1, 2).reshape(n_streams, t_len, n_heads, head_dim)

    packed_pairs = 4 if dec_seq <= ATT_TQ // 2 and (hd // LANES) % 4 == 0 else 0
    sample_q = ATT_TQ // 2 if packed_pairs else -(-dec_seq // ATT_TQ) * ATT_TQ
    sample_keys = past_len + ATT_TK
    assert sample_q <= ATT_TK

    slabs = [x_prompt.reshape(n_prompt, d), x_sample.reshape(n_sample, d)]
    new_pool_prompt, new_pool_sample = [], []
    ks = vs = logf128 = k_bf = vt = e_prompt = e_sample = k_bf_sample = vt_sample = None
    for l in range(depth):
        last = l == depth - 1
        attn = None
        if l < n_a:
            x = _ffn(slabs, ln_ffn1[l], win, wout, (l, 0))
            hist_p = jnp.zeros((batch, POOL_HIST_ROWS, d), _F32)
            hist_s = jnp.pad(cache_pool[l], ((0, 0), (POOL_HIST_ROWS - POOL_HIST, 0), (0, 0)))
            xp, pool_p = _pool_mix(x, hist_p, ln_mix[l], wpool[l], pool_scale[l],
                                   n_streams=batch, t_len=seq, row0=0, pos0=0)
            xs, pool_s = _pool_mix(x, hist_s, ln_mix[l], wpool[l], pool_scale[l],
                                   n_streams=dec_batch, t_len=dec_seq, row0=n_prompt, pos0=past_len)
            slabs = [xp, xs]
            new_pool_prompt.append(pool_p[:, POOL_HIST_ROWS - POOL_HIST:])
            new_pool_sample.append(pool_s[:, POOL_HIST_ROWS - POOL_HIST:])
        else:
            j = l - n_a
            x, q_bf = _ffn(slabs, ln_ffn1[l], win, wout, (l, 0),
                           query=(ln_mix[l], wq[j], head_dim ** -0.5 * LOG2_E))
            o_prompt = _attention(q_bf, k_bf, e_prompt, vt, n_streams=batch, t_q=seq, t_k=seq, q_off=0,
                                  q_row0=0, k_row0=0, head_dim=head_dim)
            q_s = jnp.pad(q_bf[n_prompt:].reshape(dec_batch, dec_seq, hd),
                          ((0, 0), (0, sample_q - dec_seq), (0, 0))).reshape(dec_batch * sample_q, hd)
            o_sample = _attention(q_s, k_bf_sample, e_sample, vt_sample, n_streams=dec_batch, t_q=sample_q,
                                  t_k=sample_keys, q_off=past_len, q_row0=0, k_row0=0, head_dim=head_dim,
                                  packed_pairs=packed_pairs)
            o_sample = o_sample.reshape(hd, dec_batch, sample_q)[:, :, :dec_seq].reshape(hd, n_sample)
            slabs, attn = [x], ([o_prompt, o_sample], wo[j])
        x = _ffn(slabs, ln_ffn2[l], win, wout, (l, 1), attn=attn, g_final=ln_final if last else None,
                 out_rows=rows if last else None)
        slabs = x if last else [x]
        if l == n_a - 1:
            ks, vs, logf128, k_bf, vt = _kv_project(x, [(batch, seq), (dec_batch, dec_seq)], ln_kv, wkv,
                                                    wfg, bfg, n_heads)
            e_prompt = _decay_features(logf128[:n_prompt].reshape(batch, seq, LANES), n_heads)
            logf_s = jnp.concatenate([
                jnp.pad(cache_logf, ((0, 0), (0, 0), (0, LANES - n_heads))),
                logf128[n_prompt:].reshape(dec_batch, dec_seq, LANES),
                jnp.zeros((dec_batch, ATT_TK - dec_seq, LANES), _F32)], axis=1)
            e_sample = _decay_features(logf_s, n_heads)
            k_bf_sample, vt_sample = _cache_keys(
                time_minor(cache_k), time_minor(cache_v), k_bf, vt, new_row0=n_prompt, dec_seq=dec_seq)

    y_prompt, y_sample = slabs
    logf = logf128[:, :n_heads]
    return (y_prompt.reshape(batch, seq, d), y_sample.reshape(dec_batch, dec_seq, d),
            jnp.stack(new_pool_prompt), jnp.stack(new_pool_sample),
            time_major(ks[0], batch, seq), time_major(vs[0], batch, seq),
            logf[:n_prompt].reshape(batch, seq, n_heads),
            time_major(ks[1], dec_batch, dec_seq), time_major(vs[1], dec_batch, dec_seq),
            logf[n_prompt:].reshape(dec_batch, dec_seq, n_heads))
```

```python
import functools

import jax
import jax.numpy as jnp
from jax import lax
from jax.experimental import pallas as pl
from jax.experimental.pallas import tpu as pltpu

RMS_EPS = 1e-6
POOL_WINDOWS = (2, 4, 8, 16)
POOL_HIST = max(POOL_WINDOWS) - 1

LANES = 128
TOKEN_TILE = 512
FFN_CHUNK = 256
POOL_PAD = 8
POOL_HIST_ROWS = 16
CUMSUM_CHUNK = 256
ATT_TQ = 256
ATT_TK = 512
SUM_ROWS = 16
MASK_VALUE = -1e30
LOG2_E = 1.4426950408889634
VMEM_LIMIT = 52 * 1024 * 1024

_BF16 = jnp.bfloat16
_F32 = jnp.float32


def _rms(x, g):
    ms = jnp.mean(x * x, axis=-1, keepdims=True)
    return x * lax.rsqrt(ms + RMS_EPS) * g


def _dot(a, b):
    return jnp.dot(a, b, preferred_element_type=_F32)


def _split3(x):
    hi = x.astype(_BF16)
    r = x - hi.astype(_F32)
    mid = r.astype(_BF16)
    lo = (r - mid.astype(_F32)).astype(_BF16)
    return hi, mid, lo


def _params(*sem):
    return pltpu.CompilerParams(dimension_semantics=sem, vmem_limit_bytes=VMEM_LIMIT)


def _const_spec(shape):
    zeros = (0,) * len(shape)
    return pl.BlockSpec(shape, lambda *_: zeros, pipeline_mode=pl.Buffered(1))


def _slab_blocks(rows):
    assert all(n % TOKEN_TILE == 0 for n in rows)
    return [n // TOKEN_TILE for n in rows]


def _slab_specs(rows, width, transposed=False):
    tail = width if isinstance(width, tuple) else (width,)
    specs, start = [], 0
    for nb in _slab_blocks(rows):
        def index(i, start=start, nb=nb):
            blk = jnp.clip(i - start, 0, nb - 1)
            return (0, blk) if transposed else (blk,) + (0,) * len(tail)
        shape = tail + (TOKEN_TILE,) if transposed else (TOKEN_TILE,) + tail
        mode = dict(pipeline_mode=pl.Buffered(1)) if nb == 1 else {}
        specs.append(pl.BlockSpec(shape, index, **mode))
        start += nb
    return specs


def _slab_read(refs, rows):
    i = pl.program_id(0)
    blocks = _slab_blocks(rows)
    value, end = refs[-1][...], sum(blocks)
    for ref, nb in zip(refs[-2::-1], blocks[:0:-1]):
        end -= nb
        value = jnp.where(i < end, ref[...], value)
    return value


def _slab_write(refs, rows, value):
    i = pl.program_id(0)
    start = 0
    for ref, nb in zip(refs, _slab_blocks(rows)):
        @pl.when((i >= start) & (i < start + nb))
        def _(ref=ref):
            ref[...] = value
        start += nb


def _slab_shapes(rows, width, dtype):
    tail = width if isinstance(width, tuple) else (width,)
    return [jax.ShapeDtypeStruct((n,) + tail, dtype) for n in rows]


def _ffn_body(*refs, rows_in, rows_attn, rows_out, d_ff, final_norm, query_scale):
    refs = list(refs)
    take = lambda n: [refs.pop(0) for _ in range(n)]
    x_refs = take(len(rows_in))
    if rows_attn:
        ot_refs, (wo_ref,) = take(len(rows_attn)), take(1)
    g_ref, win_ref, wout_ref = take(3)
    if final_norm:
        (gf_ref,) = take(1)
    if query_scale is not None:
        gq_ref, wq_ref = take(2)
    o_refs = take(len(rows_out))
    if query_scale is not None:
        (q_ref,) = take(1)
    (acc_ref,) = refs

    x = _slab_read(x_refs, rows_in)
    if rows_attn:
        o_t = _slab_read(ot_refs, rows_attn)
        x = x + lax.dot_general(o_t, wo_ref[...], (((0,), (0,)), ((), ())), preferred_element_type=_F32)
    xn = _rms(x, g_ref[...]).astype(_BF16)
    for c in range(d_ff // FFN_CHUNK):
        lo, hi = c * FFN_CHUNK, (c + 1) * FFN_CHUNK
        g = _dot(xn, win_ref[:, lo:hi].astype(_BF16))
        u = _dot(xn, win_ref[:, d_ff + lo:d_ff + hi].astype(_BF16))
        a = (g * (1.0 / (1.0 + jnp.exp(-g))) * u).astype(_BF16)
        part = _dot(a, wout_ref[lo:hi, :].astype(_BF16))
        if c == 0:
            acc_ref[...] = part
        else:
            acc_ref[...] += part
    y = x + 0.5 * acc_ref[...]
    if query_scale is not None:
        u_q = _rms(y, gq_ref[...]).astype(_BF16)
        q_ref[...] = (_dot(u_q, wq_ref[...]) * query_scale).astype(_BF16)
    if final_norm:
        y = _rms(y, gf_ref[...])
    _slab_write(o_refs, rows_out, y)


def _ffn(xs, g, w_in_all, w_out_all, which, *, attn=None, query=None, g_final=None, out_rows=None):
    d = xs[0].shape[1]
    d_ff = w_out_all.shape[2]
    assert d_ff % FFN_CHUNK == 0
    rows_in = tuple(x.shape[0] for x in xs)
    rows_out = tuple(out_rows) if out_rows else (sum(rows_in),)
    assert sum(rows_out) == sum(rows_in)
    n = sum(rows_in)
    picked = lambda w: pl.BlockSpec((None, None) + w.shape[2:], lambda i: tuple(which) + (0, 0),
                                    pipeline_mode=pl.Buffered(1))
    in_specs, args, rows_attn = _slab_specs(rows_in, d), list(xs), ()
    if attn is not None:
        o_ts, w_o = attn
        rows_attn = tuple(o.shape[1] for o in o_ts)
        assert sum(rows_attn) == n
        in_specs += _slab_specs(rows_attn, w_o.shape[0], transposed=True) + [_const_spec(w_o.shape)]
        args += list(o_ts) + [w_o]
    in_specs += [_const_spec((1, d)), picked(w_in_all), picked(w_out_all)]
    args += [g.reshape(1, d), w_in_all, w_out_all]
    if g_final is not None:
        in_specs.append(_const_spec((1, d)))
        args.append(g_final.reshape(1, d))
    out_specs, out_shape = _slab_specs(rows_out, d), _slab_shapes(rows_out, d, _F32)
    if query is not None:
        g_q, w_q, scale = query
        in_specs += [_const_spec((1, d)), _const_spec(w_q.shape)]
        args += [g_q.reshape(1, d), w_q]
        out_specs = out_specs + [pl.BlockSpec((TOKEN_TILE, w_q.shape[1]), lambda i: (i, 0))]
        out_shape = out_shape + [jax.ShapeDtypeStruct((n, w_q.shape[1]), _BF16)]
    out = pl.pallas_call(
        functools.partial(_ffn_body, rows_in=rows_in, rows_attn=rows_attn, rows_out=rows_out, d_ff=d_ff,
                          final_norm=g_final is not None, query_scale=None if query is None else query[2]),
        grid=(n // TOKEN_TILE,),
        in_specs=in_specs,
        out_specs=out_specs,
        out_shape=out_shape,
        scratch_shapes=[pltpu.VMEM((TOKEN_TILE, d), _F32)],
        compiler_params=_params("arbitrary"),
        name="ffn",
    )(*args)
    y = out[:len(rows_out)] if out_rows else out[0]
    return (y, out[-1]) if query is not None else y


def _pool_body(x_ref, hist_ref, g_ref, wp_ref, sc_ref, o_ref, pool_ref, buf, s2, s4, s8, *, tt, pos0):
    t = pl.program_id(1)
    d = x_ref.shape[-1]
    gw = d // len(POOL_WINDOWS)
    off = POOL_PAD + POOL_HIST_ROWS
    end = off + tt

    @pl.when(t == 0)
    def _():
        zeros = jnp.zeros((POOL_PAD, d), _F32)
        buf[0:POOL_PAD, :] = zeros
        s2[0:POOL_PAD, :] = zeros
        s4[0:POOL_PAD, :] = zeros
        buf[POOL_PAD:off, :] = hist_ref[...]

    @pl.when(t > 0)
    def _():
        buf[POOL_PAD:off, :] = buf[end - POOL_HIST_ROWS:end, :]

    x = x_ref[...]
    u = _rms(x, g_ref[...])
    buf[off:end, :] = u

    s2[POOL_PAD:end, :] = buf[POOL_PAD:end, :] + buf[POOL_PAD - 1:end - 1, :]
    s4[POOL_PAD:end, gw:] = s2[POOL_PAD:end, gw:] + s2[POOL_PAD - 2:end - 2, gw:]
    s8[POOL_PAD:end, 2 * gw:] = s4[POOL_PAD:end, 2 * gw:] + s4[POOL_PAD - 4:end - 4, 2 * gw:]
    sums = (
        s2[off:end, 0:gw],
        s4[off:end, gw:2 * gw],
        s8[off:end, 2 * gw:3 * gw],
        s8[off:end, 3 * gw:] + s8[off - 8:end - 8, 3 * gw:],
    )
    pos = pos0 + t * tt + lax.broadcasted_iota(jnp.int32, (tt, 1), 0)
    for gi, w in enumerate(POOL_WINDOWS):
        sl = slice(gi * gw, (gi + 1) * gw)
        inv_cnt = 1.0 / jnp.minimum(pos + 1, w).astype(_F32)
        diff = (sums[gi] * inv_cnt - u[:, sl]).astype(_BF16)
        y = _dot(diff, wp_ref[gi])
        o_ref[:, sl] = x[:, sl] + y * sc_ref[:, sl]

    @pl.when(t == pl.num_programs(1) - 1)
    def _():
        pool_ref[...] = buf[end - POOL_HIST_ROWS:end, :]


def _pool_mix(x_all, hist16, g, w_pool, scale, *, n_streams, t_len, row0, pos0):
    d = x_all.shape[1]
    tt = min(TOKEN_TILE, t_len)
    assert t_len % tt == 0 and row0 % tt == 0
    nt = t_len // tt
    blk0 = row0 // tt
    hist = pl.BlockSpec((None, POOL_HIST_ROWS, d), lambda b, t: (b, 0, 0))
    rows_buf = POOL_PAD + POOL_HIST_ROWS + tt
    return pl.pallas_call(
        functools.partial(_pool_body, tt=tt, pos0=pos0),
        grid=(n_streams, nt),
        in_specs=[pl.BlockSpec((tt, d), lambda b, t: (blk0 + b * nt + t, 0)), hist, _const_spec((1, d)),
                  _const_spec(w_pool.shape), _const_spec((1, d))],
        out_specs=[pl.BlockSpec((tt, d), lambda b, t: (b * nt + t, 0)), hist],
        out_shape=[jax.ShapeDtypeStruct((n_streams * t_len, d), _F32),
                   jax.ShapeDtypeStruct((n_streams, POOL_HIST_ROWS, d), _F32)],
        scratch_shapes=[pltpu.VMEM((rows_buf, d), _F32)] * 4,
        compiler_params=_params("arbitrary", "arbitrary"),
        name="pool_mix",
    )(x_all, hist16, g.reshape(1, d), w_pool, scale.reshape(1, d))


def _kv_body(x_ref, g_ref, wkv_ref, wf_ref, bf_ref, *out_refs, rows, hd, n_heads):
    ns = len(rows)
    kt_refs, vt_refs = out_refs[:ns], out_refs[ns:2 * ns]
    lf_ref, kb_ref, vtile_ref = out_refs[2 * ns:]
    xn = _rms(x_ref[...], g_ref[...]).astype(_BF16)
    kv = _dot(xn, wkv_ref[...])
    kv_t = kv.T
    _slab_write(kt_refs, rows, kv_t[0:hd, :])
    _slab_write(vt_refs, rows, kv_t[hd:2 * hd, :])
    kb_ref[...] = kv[:, 0:hd].astype(_BF16)
    vtile_ref[...] = kv_t[hd:2 * hd, :].astype(_BF16)
    z = _dot(xn, wf_ref[...]) + bf_ref[...]
    logf = jnp.minimum(z, 0.0) - jnp.log1p(jnp.exp(-jnp.abs(z)))
    lane = lax.broadcasted_iota(jnp.int32, logf.shape, 1)
    lf_ref[...] = jnp.where(lane < n_heads, logf, 0.0)


def _kv_project(x, streams, g, w_kv, w_fg_pad, b_fg_pad, n_heads):
    n, d = x.shape
    hd = w_kv.shape[1] // 2
    nb = n // TOKEN_TILE
    rows = tuple(s * t for s, t in streams)
    shapes = [(s, t) if t % TOKEN_TILE == 0 else (1, s * t) for s, t in streams]
    row = lambda width: pl.BlockSpec((TOKEN_TILE, width), lambda i: (i, 0))

    def transposed_specs():
        specs, start = [], 0
        for (_, cols), nblk in zip(shapes, _slab_blocks(rows)):
            def index(i, start=start, nblk=nblk, nt=cols // TOKEN_TILE):
                blk = jnp.clip(i - start, 0, nblk - 1)
                return (blk // nt, 0, blk % nt)
            mode = dict(pipeline_mode=pl.Buffered(1)) if nblk == 1 else {}
            specs.append(pl.BlockSpec((None, hd, TOKEN_TILE), index, **mode))
            start += nblk
        return specs

    transposed_shapes = [jax.ShapeDtypeStruct((s, hd, cols), _F32) for s, cols in shapes]
    out = pl.pallas_call(
        functools.partial(_kv_body, rows=rows, hd=hd, n_heads=n_heads),
        grid=(nb,),
        in_specs=[row(d), _const_spec((1, d)), _const_spec(w_kv.shape), _const_spec(w_fg_pad.shape),
                  _const_spec((1, LANES))],
        out_specs=transposed_specs() + transposed_specs() + [
            row(LANES), row(hd), pl.BlockSpec((None, hd, TOKEN_TILE), lambda i: (i, 0, 0))],
        out_shape=transposed_shapes + transposed_shapes + [
            jax.ShapeDtypeStruct((n, LANES), _F32), jax.ShapeDtypeStruct((n, hd), _BF16),
            jax.ShapeDtypeStruct((nb, hd, TOKEN_TILE), _BF16)],
        compiler_params=_params("arbitrary"),
        name="kv_project",
    )(x, g.reshape(1, d), w_kv, w_fg_pad, b_fg_pad)
    ns = len(rows)
    return out[:ns], out[ns:2 * ns], out[2 * ns], out[2 * ns + 1], out[2 * ns + 2]


def _cache_body(ckt_ref, cvt_ref, kn_ref, vn_ref, k_ref, vt_ref, *, n_cache_blocks, dec_seq, new_row0):
    b = pl.program_id(0)
    j = pl.program_id(1)

    @pl.when(j < n_cache_blocks)
    def _():
        k_ref[...] = ckt_ref[...].T.astype(_BF16)
        vt_ref[...] = cvt_ref[...].astype(_BF16)

    @pl.when(j >= n_cache_blocks)
    def _():
        hd = k_ref.shape[1]
        pad = jnp.zeros((ATT_TK - dec_seq, hd), _F32)
        k_ref[...] = jnp.concatenate([kn_ref[...].astype(_F32), pad], axis=0).astype(_BF16)
        off = (new_row0 + b * dec_seq) % TOKEN_TILE
        r = lax.broadcasted_iota(jnp.int32, (TOKEN_TILE, ATT_TK), 0)
        c = lax.broadcasted_iota(jnp.int32, (TOKEN_TILE, ATT_TK), 1)
        select = jnp.where((r == off + c) & (c < dec_seq), 1.0, 0.0).astype(_BF16)
        vt_ref[...] = _dot(vn_ref[...], select).astype(_BF16)


def _cache_keys(cache_kt, cache_vt, k_new_bf, vt_new, *, new_row0, dec_seq):
    s, hd, past = cache_kt.shape
    assert past % ATT_TK == 0 and dec_seq % 16 == 0 and TOKEN_TILE % dec_seq == 0 and new_row0 % dec_seq == 0
    ncb = past // ATT_TK
    nkb = ncb + 1
    cache = pl.BlockSpec((None, hd, ATT_TK), lambda b, j: (b, 0, jnp.minimum(j, ncb - 1)))
    return pl.pallas_call(
        functools.partial(_cache_body, n_cache_blocks=ncb, dec_seq=dec_seq, new_row0=new_row0),
        grid=(s, nkb),
        in_specs=[cache, cache,
                  pl.BlockSpec((dec_seq, hd), lambda b, j: (new_row0 // dec_seq + b, 0)),
                  pl.BlockSpec((None, hd, TOKEN_TILE), lambda b, j: ((new_row0 + b * dec_seq) // TOKEN_TILE, 0, 0))],
        out_specs=[pl.BlockSpec((ATT_TK, hd), lambda b, j: (b * nkb + j, 0)),
                   pl.BlockSpec((None, hd, ATT_TK), lambda b, j: (b * nkb + j, 0, 0))],
        out_shape=[jax.ShapeDtypeStruct((s * nkb * ATT_TK, hd), _BF16),
                   jax.ShapeDtypeStruct((s * nkb, hd, ATT_TK), _BF16)],
        compiler_params=_params("arbitrary", "arbitrary"),
        name="cache_keys",
    )(cache_kt, cache_vt, k_new_bf, vt_new)


def _decay_body(lf_ref, e_ref, c_ref, *, n_heads):
    n_chunks = lf_ref.shape[0] // CUMSUM_CHUNK
    r = lax.broadcasted_iota(jnp.int32, (CUMSUM_CHUNK, CUMSUM_CHUNK), 0)
    c = lax.broadcasted_iota(jnp.int32, (CUMSUM_CHUNK, CUMSUM_CHUNK), 1)
    tril = jnp.where(r >= c, 1.0, 0.0).astype(_BF16)
    sr = lax.broadcasted_iota(jnp.int32, (LANES, LANES), 0)
    sc = lax.broadcasted_iota(jnp.int32, (LANES, LANES), 1)
    place = [jnp.where((sc == 3 * sr + j) & (sr < n_heads), 1.0, 0.0).astype(_BF16) for j in range(3)]

    for i in range(n_chunks):
        rows = slice(i * CUMSUM_CHUNK, (i + 1) * CUMSUM_CHUNK)
        hi, mid, lo = _split3(lf_ref[rows, :])
        c_ref[rows, :] = _dot(tril, hi) + _dot(tril, mid) + _dot(tril, lo)
    carry = jnp.zeros((1, LANES), _F32)
    for i in range(n_chunks):
        rows = slice(i * CUMSUM_CHUNK, (i + 1) * CUMSUM_CHUNK)
        csum = c_ref[rows, :] + carry
        carry = carry + c_ref[(i + 1) * CUMSUM_CHUNK - 1:(i + 1) * CUMSUM_CHUNK, :]
        p0, p1, p2 = _split3(-LOG2_E * csum)
        feats = _dot(p0, place[0]) + _dot(p1, place[1]) + _dot(p2, place[2])
        e_ref[rows, :] = feats.astype(_BF16)


def _decay_features(logf, n_heads):
    b, t, _ = logf.shape
    assert t % CUMSUM_CHUNK == 0 and 3 * n_heads <= LANES
    blk = pl.BlockSpec((None, t, LANES), lambda i: (i, 0, 0))
    return pl.pallas_call(
        functools.partial(_decay_body, n_heads=n_heads),
        grid=(b,),
        in_specs=[blk],
        out_specs=blk,
        out_shape=jax.ShapeDtypeStruct((b, t, LANES), _BF16),
        scratch_shapes=[pltpu.VMEM((t, LANES), _F32)],
        compiler_params=_params("arbitrary"),
        name="decay_features",
    )(logf)


def _attn_group(n_sub, q_off):
    return ATT_TK // ATT_TQ if (n_sub * ATT_TQ) % ATT_TK == 0 and q_off % ATT_TK == 0 else 1


def _key_block_pipeline(n_full, scores, consume, state):
    def double_step(j, carry):
        maxima, state = carry
        prev = jnp.where(j == 0, n_full, j - 1)
        maxima_a = scores(j, 1, False)
        state = consume(prev, 0, maxima, state)
        maxima_b = scores(j + 1, 0, False)
        state = consume(j, 1, maxima_a, state)
        return maxima_b, state

    def odd_tail(carry):
        maxima, state = carry
        prev = jnp.where(n_full == 1, n_full, n_full - 2)
        maxima_a = scores(n_full - 1, 1, False)
        state = consume(prev, 0, maxima, state)
        return consume(n_full - 1, 1, maxima_a, state)

    def even_tail(carry):
        maxima, state = carry
        last = jnp.where(n_full == 0, n_full, n_full - 1)
        return consume(last, 0, maxima, state)

    carry = (scores(n_full, 0, True), state)
    carry = lax.fori_loop(0, n_full // 4, lambda t, c: double_step(4 * t + 2, double_step(4 * t, c)), carry)
    rest = n_full % 4
    carry = lax.cond(rest >= 2, lambda c: double_step(n_full - rest, c), lambda c: c, carry)
    return lax.cond(n_full % 2 == 1, odd_tail, even_tail, carry)


def _attn_body(q_ref, k_ref, e_ref, vt_ref, o_ref, s_ref, *, n_sub, q_off, head_dim):
    pair = pl.program_id(1)
    qblk = pl.program_id(2)
    heads_per_block = LANES // head_dim
    group = _attn_group(n_sub, q_off)
    lane = lax.broadcasted_iota(jnp.int32, (ATT_TQ, LANES), 1)
    feat = lax.broadcasted_iota(jnp.int32, (LANES, ATT_TQ), 0)
    key_iota = lax.broadcasted_iota(jnp.int32, (ATT_TK, ATT_TQ), 0)
    qry_iota = lax.broadcasted_iota(jnp.int32, (ATT_TK, ATT_TQ), 1)
    chains = [(hh, s) for hh in range(heads_per_block) for s in range(group)]
    ones_rows = jnp.ones((SUM_ROWS, ATT_TK), _BF16)

    for g in range(n_sub // group):
        q0 = q_off + (qblk * n_sub + g * group) * ATT_TQ
        q_augs = []
        for hh, s in chains:
            head = pair * heads_per_block + hh
            bias_sel = jnp.where((feat >= 3 * head) & (feat < 3 * head + 3), 1.0, 0.0).astype(_BF16)
            head_lanes = (lane >= hh * head_dim) & (lane < (hh + 1) * head_dim)
            q = q_ref[(g * group + s) * ATT_TQ:(g * group + s + 1) * ATT_TQ, :].astype(_F32)
            q_t = jnp.where(head_lanes, q, 0.0).T.astype(_BF16)
            q_augs.append(jnp.concatenate([q_t, bias_sel], axis=0))

        def scores(j, slot, masked):
            rows = pl.ds(pl.multiple_of(j * ATT_TK, ATT_TK), ATT_TK)
            k_aug = jnp.concatenate([k_ref[rows, :], e_ref[rows, :]], axis=1)
            tiles = [_dot(k_aug, q_aug) for q_aug in q_augs]
            maxima = []
            for c, ((hh, s), sc) in enumerate(zip(chains, tiles)):
                if masked:
                    sc = jnp.where(key_iota + j * ATT_TK <= qry_iota + (q0 + s * ATT_TQ), sc, MASK_VALUE)
                s_ref[slot, c] = sc
                maxima.append(jnp.max(sc, axis=0, keepdims=True))
            return tuple(maxima)

        def consume(j, slot, maxima, state):
            out = []
            for c, ((hh, s), bm, (m, acc)) in enumerate(zip(chains, maxima, state)):
                m_new = jnp.maximum(m, bm)
                alpha = jnp.exp2(m - m_new)
                vt = jnp.concatenate([vt_ref[j, hh * head_dim:(hh + 1) * head_dim, :], ones_rows], axis=0)
                acc = alpha * acc
                for lo in range(0, ATT_TK, ATT_TK // 2):
                    keys = slice(lo, lo + ATT_TK // 2)
                    p = jnp.exp2(s_ref[slot, c, keys, :] - m_new).astype(_BF16)
                    acc = acc + _dot(vt[:, keys], p)
                out.append((m_new, acc))
            return tuple(out)

        init = (jnp.full((1, ATT_TQ), MASK_VALUE, _F32), jnp.zeros((head_dim + SUM_ROWS, ATT_TQ), _F32))
        final = _key_block_pipeline(q0 // ATT_TK, scores, consume, (init,) * len(chains))
        for (hh, s), (_, acc) in zip(chains, final):
            cols = slice((g * group + s) * ATT_TQ, (g * group + s + 1) * ATT_TQ)
            out = acc[0:head_dim, :] / acc[head_dim:head_dim + 1, :]
            o_ref[hh * head_dim:(hh + 1) * head_dim, cols] = out.astype(_BF16)


def _attn_packed_body(q_ref, k_ref, e_ref, vt_ref, o_ref, s_ref, *, pairs, q_off, head_dim):
    pair0 = pl.program_id(1) * pairs
    half = ATT_TQ // 2
    q0 = q_off
    feat = lax.broadcasted_iota(jnp.int32, (LANES, ATT_TQ), 0)
    col_head = lax.broadcasted_iota(jnp.int32, (LANES, ATT_TQ), 1) // half
    key_iota = lax.broadcasted_iota(jnp.int32, (ATT_TK, ATT_TQ), 0)
    qry_iota = lax.broadcasted_iota(jnp.int32, (ATT_TK, ATT_TQ), 1) % half
    ones_rows = jnp.ones((SUM_ROWS, ATT_TK), _BF16)

    q_augs = []
    for pp in range(pairs):
        q_t = q_ref[:, pp * LANES:(pp + 1) * LANES].astype(_F32).T
        q_t = jnp.concatenate([q_t, q_t], axis=1)
        q_t = jnp.where(feat // head_dim == col_head, q_t, 0.0).astype(_BF16)
        head = (pair0 + pp) * (LANES // head_dim) + col_head
        bias_sel = jnp.where((feat >= 3 * head) & (feat < 3 * head + 3), 1.0, 0.0).astype(_BF16)
        q_augs.append(jnp.concatenate([q_t, bias_sel], axis=0))

    def scores(j, slot, masked):
        rows = pl.ds(pl.multiple_of(j * ATT_TK, ATT_TK), ATT_TK)
        tiles = [_dot(jnp.concatenate([k_ref[rows, pp * LANES:(pp + 1) * LANES], e_ref[rows, :]], axis=1),
                      q_augs[pp]) for pp in range(pairs)]
        maxima = []
        for pp, sc in enumerate(tiles):
            if masked:
                sc = jnp.where(key_iota + j * ATT_TK <= qry_iota + q0, sc, MASK_VALUE)
            s_ref[slot, pp] = sc
            maxima.append(jnp.max(sc, axis=0, keepdims=True))
        return tuple(maxima)

    def consume(j, slot, maxima, state):
        out = []
        for pp, (bm, (m, acc)) in enumerate(zip(maxima, state)):
            m_new = jnp.maximum(m, bm)
            alpha = jnp.exp2(m - m_new)
            p = jnp.exp2(s_ref[slot, pp] - m_new).astype(_BF16)
            vt = jnp.concatenate([vt_ref[j, pp * LANES:(pp + 1) * LANES, :], ones_rows], axis=0)
            out.append((m_new, alpha * acc + _dot(vt, p)))
        return tuple(out)

    init = (jnp.full((1, ATT_TQ), MASK_VALUE, _F32), jnp.zeros((LANES + SUM_ROWS, ATT_TQ), _F32))
    final = _key_block_pipeline(jnp.int32(q0 // ATT_TK), scores, consume, (init,) * pairs)
    for pp, (_, acc) in enumerate(final):
        for hh in range(LANES // head_dim):
            cols = slice(hh * half, (hh + 1) * half)
            out = acc[hh * head_dim:(hh + 1) * head_dim, cols] / acc[LANES:LANES + 1, cols]
            o_ref[pp * LANES + hh * head_dim:pp * LANES + (hh + 1) * head_dim, :] = out.astype(_BF16)


def _attention(q, k, e, vt, *, n_streams, t_q, t_k, q_off, q_row0, k_row0, head_dim, packed_pairs=0):
    hd = q.shape[1]
    n_pairs = hd // LANES
    nkb = t_k // ATT_TK
    kb0, vb0 = k_row0 // t_k, k_row0 // ATT_TK
    assert t_k % ATT_TK == 0 and k_row0 % t_k == 0 and vb0 % nkb == 0 and q_off + t_q <= t_k
    if packed_pairs:
        width = packed_pairs * LANES
        assert t_q == ATT_TQ // 2 and q_off % ATT_TK == 0 and n_pairs % packed_pairs == 0 and q_row0 % t_q == 0
        body = functools.partial(_attn_packed_body, pairs=packed_pairs, q_off=q_off, head_dim=head_dim)
        grid, tq_blk, n_chains, nq = (n_streams, n_pairs // packed_pairs, 1), t_q, packed_pairs, 1
    else:
        width = LANES
        tq_blk = min(t_q, 4 * ATT_TQ)
        assert t_q % tq_blk == 0 and tq_blk % ATT_TQ == 0 and q_off % ATT_TQ == 0 and q_row0 % tq_blk == 0
        nq = t_q // tq_blk
        n_sub = tq_blk // ATT_TQ
        body = functools.partial(_attn_body, n_sub=n_sub, q_off=q_off, head_dim=head_dim)
        grid, n_chains = (n_streams, n_pairs, nq), (LANES // head_dim) * _attn_group(n_sub, q_off)
    qb0 = q_row0 // tq_blk
    return pl.pallas_call(
        body,
        grid=grid,
        in_specs=[
            pl.BlockSpec((tq_blk, width), lambda b, p, i: (qb0 + b * nq + i, p)),
            pl.BlockSpec((t_k, width), lambda b, p, i: (kb0 + b, p)),
            pl.BlockSpec((None, t_k, LANES), lambda b, p, i: (b, 0, 0)),
            pl.BlockSpec((nkb, width, ATT_TK), lambda b, p, i: (vb0 // nkb + b, p, 0)),
        ],
        out_specs=pl.BlockSpec((width, tq_blk), lambda b, p, i: (p, b * nq + i)),
        out_shape=jax.ShapeDtypeStruct((hd, n_streams * t_q), _BF16),
        scratch_shapes=[pltpu.VMEM((2, n_chains, ATT_TK, ATT_TQ), _F32)],
        compiler_params=_params("arbitrary", "arbitrary", "arbitrary"),
        name="forget_attention",
    )(q, k, e, vt)


def kernel(x_prompt, x_sample, cache_pool, cache_k, cache_v, cache_logf, ln_ffn1, ln_mix, ln_ffn2, w_ffn_in, w_ffn_out, w_pool, pool_scale, ln_kv, w_kv, w_fgate, b_fgate, w_q, w_o, ln_final):
    batch, seq, d = x_prompt.shape
    dec_batch, dec_seq, _ = x_sample.shape
    past_len, n_heads, head_dim = cache_k.shape[1:]
    hd = n_heads * head_dim
    depth = ln_ffn1.shape[0]
    n_a = w_pool.shape[0]
    n_prompt, n_sample = batch * seq, dec_batch * dec_seq
    rows = (n_prompt, n_sample)
    assert head_dim * 2 == LANES and seq % TOKEN_TILE == 0 and TOKEN_TILE == ATT_TK

    win, wout = w_ffn_in, w_ffn_out
    wpool = w_pool.astype(_BF16)
    wkv = w_kv.astype(_BF16)
    wq = w_q.astype(_BF16)
    wo = w_o.astype(_BF16)
    wfg = jnp.pad(w_fgate, ((0, 0), (0, LANES - n_heads))).astype(_BF16)
    bfg = jnp.pad(b_fgate, (0, LANES - n_heads)).reshape(1, LANES)

    def time_minor(cache):
        return cache.transpose(0, 2, 3, 1).reshape(cache.shape[0], hd, cache.shape[1])

    def time_major(x_t, n_streams, t_len):
        g = x_t.shape[0]
        x_t = x_t.reshape(g, n_heads, head_dim, n_streams // g, t_len)
        return x_t.transpose(0, 3, 4, 1, 2).reshape(n_streams, t_len, n_heads, head_dim)

    packed_pairs = 4 if dec_seq <= ATT_TQ // 2 and (hd // LANES) % 4 == 0 else 0
    sample_q = ATT_TQ // 2 if packed_pairs else -(-dec_seq // ATT_TQ) * ATT_TQ
    sample_keys = past_len + ATT_TK
    assert sample_q <= ATT_TK

    slabs = [x_prompt.reshape(n_prompt, d), x_sample.reshape(n_sample, d)]
    new_pool_prompt, new_pool_sample = [], []
    ks = vs = logf128 = k_bf = vt = e_prompt = e_sample = k_bf_sample = vt_sample = None
    for l in range(depth):
        last = l == depth - 1
        attn = None
        if l < n_a:
            x = _ffn(slabs, ln_ffn1[l], win, wout, (l, 0))
            hist_p = jnp.zeros((batch, POOL_HIST_ROWS, d), _F32)
            hist_s = jnp.pad(cache_pool[l], ((0, 0), (POOL_HIST_ROWS - POOL_HIST, 0), (0, 0)))
            xp, pool_p = _pool_mix(x, hist_p, ln_mix[l], wpool[l], pool_scale[l],
                                   n_streams=batch, t_len=seq, row0=0, pos0=0)
            xs, pool_s = _pool_mix(x, hist_s, ln_mix[l], wpool[l], pool_scale[l],
                                   n_streams=dec_batch, t_len=dec_seq, row0=n_prompt, pos0=past_len)
            slabs = [xp, xs]
            new_pool_prompt.append(pool_p[:, POOL_HIST_ROWS - POOL_HIST:])
            new_pool_sample.append(pool_s[:, POOL_HIST_ROWS - POOL_HIST:])
        else:
            j = l - n_a
            x, q_bf = _ffn(slabs, ln_ffn1[l], win, wout, (l, 0),
                           query=(ln_mix[l], wq[j], head_dim ** -0.5 * LOG2_E))
            o_prompt = _attention(q_bf, k_bf, e_prompt, vt, n_streams=batch, t_q=seq, t_k=seq, q_off=0,
                                  q_row0=0, k_row0=0, head_dim=head_dim)
            q_s = jnp.pad(q_bf[n_prompt:].reshape(dec_batch, dec_seq, hd),
                          ((0, 0), (0, sample_q - dec_seq), (0, 0))).reshape(dec_batch * sample_q, hd)
            o_sample = _attention(q_s, k_bf_sample, e_sample, vt_sample, n_streams=dec_batch, t_q=sample_q,
                                  t_k=sample_keys, q_off=past_len, q_row0=0, k_row0=0, head_dim=head_dim,
                                  packed_pairs=packed_pairs)
            o_sample = o_sample.reshape(hd, dec_batch, sample_q)[:, :, :dec_seq].reshape(hd, n_sample)
            slabs, attn = [x], ([o_prompt, o_sample], wo[j])
        x = _ffn(slabs, ln_ffn2[l], win, wout, (l, 1), attn=attn, g_final=ln_final if last else None,
                 out_rows=rows if last else None)
        slabs = x if last else [x]
        if l == n_a - 1:
            ks, vs, logf128, k_bf, vt = _kv_project(x, [(batch, seq), (dec_batch, dec_seq)], ln_kv, wkv,
                                                    wfg, bfg, n_heads)
            e_prompt = _decay_features(logf128[:n_prompt].reshape(batch, seq, LANES), n_heads)
            logf_s = jnp.concatenate([
                jnp.pad(cache_logf, ((0, 0), (0, 0), (0, LANES - n_heads))),
                logf128[n_prompt:].reshape(dec_batch, dec_seq, LANES),
                jnp.zeros((dec_batch, ATT_TK - dec_seq, LANES), _F32)], axis=1)
            e_sample = _decay_features(logf_s, n_heads)
            k_bf_sample, vt_sample = _cache_keys(
                time_minor(cache_k), time_minor(cache_v), k_bf, vt, new_row0=n_prompt, dec_seq=dec_seq)

    y_prompt, y_sample = slabs
    logf = logf128[:, :n_heads]
    return (y_prompt.reshape(batch, seq, d), y_sample.reshape(dec_batch, dec_seq, d),
            jnp.stack(new_pool_prompt), jnp.stack(new_pool_sample),
            time_major(ks[0], batch, seq), time_major(vs[0], batch, seq),
            logf[:n_prompt].reshape(batch, seq, n_heads),
            time_major(ks[1], dec_batch, dec_seq), time_major(vs[1], dec_batch, dec_seq),
            logf[n_prompt:].reshape(dec_batch, dec_seq, n_heads))
```
